```python
import jax, jax.numpy as jnp
from jax import lax
import numpy as np

D_MODEL = 4096
BATCH = 1
SEQ = 16384
DEPTH = 1

GRID_W = 64
CHUNK = 128
GM_WIDTH = D_MODEL // 2
GM_GROUP_DIM = 128
GM_GROUPS = GM_WIDTH // GM_GROUP_DIM
NA_WIDTH = D_MODEL // 2
NA_HEAD_DIM = 128
NA_HEADS = NA_WIDTH // NA_HEAD_DIM
NA_WIN_H_MAX = 8
NA_WIN_W = 16
D_FF = ((-(-8 * D_MODEL // 3) + 255) // 256) * 256
RMS_EPS = 1e-6
LN_EPS = 1e-5

kernel_name = "hybrid_gmlp_natten_gated_block"


def rms_norm(x, g):
    xf = x.astype(jnp.float32)
    y = xf * lax.rsqrt(jnp.mean(xf * xf, axis=-1, keepdims=True) + RMS_EPS)
    return (y * g.astype(jnp.float32)).astype(x.dtype)


def layer_norm(x, g, b):
    xf = x.astype(jnp.float32)
    mu = jnp.mean(xf, axis=-1, keepdims=True)
    xc = xf - mu
    y = xc * lax.rsqrt(jnp.mean(xc * xc, axis=-1, keepdims=True) + LN_EPS)
    return (y * g.astype(jnp.float32) + b.astype(jnp.float32)).astype(x.dtype)


def gmlp_spatial_gating(u, v, ln_g, ln_b, w_s, b_s):
    B, S, _ = u.shape
    n_chunks = S // CHUNK
    u = jax.nn.gelu(u, approximate=False)
    v = layer_norm(jax.nn.gelu(v, approximate=False), ln_g, ln_b)
    v = v.reshape(B, n_chunks, CHUNK, GM_GROUPS, GM_GROUP_DIM)
    mixed = jnp.einsum("gij,bnjgc->bnigc", w_s, v) + b_s.T[None, None, :, :, None]
    return u * mixed.reshape(B, S, GM_WIDTH)


def neighbourhood_attention(q, k, v, q_gain, k_gain, rpb):
    B, S, _ = q.shape
    rows = S // GRID_W
    kh = min(NA_WIN_H_MAX, rows)
    scale = NA_HEAD_DIM ** -0.5

    def to_grid(t):
        return t.reshape(B, rows, GRID_W, NA_HEADS, NA_HEAD_DIM)

    qg = rms_norm(to_grid(q), q_gain)
    kg = rms_norm(to_grid(k), k_gain)
    vg = to_grid(v)

    cols = np.arange(GRID_W)
    col_start = np.clip(cols - NA_WIN_W // 2, 0, GRID_W - NA_WIN_W)
    col_idx = col_start[:, None] + np.arange(NA_WIN_W)[None, :]
    col_off = col_idx - cols[:, None] + (NA_WIN_W - 1)
    rpb_cols = rpb[:, :, col_off].astype(jnp.float32)

    def one_row(args):
        r, q_row = args
        rs = jnp.clip(r - kh // 2, 0, rows - kh)
        k_band = lax.dynamic_slice_in_dim(kg, rs, kh, axis=1)
        v_band = lax.dynamic_slice_in_dim(vg, rs, kh, axis=1)
        k_win = k_band[:, :, col_idx]
        v_win = v_band[:, :, col_idx]
        s = jnp.einsum("bwhd,biwjhd->bhwij", q_row, k_win).astype(jnp.float32) * scale
        row_off = rs + jnp.arange(kh) - r + (NA_WIN_H_MAX - 1)
        bias = jnp.take(rpb_cols, row_off, axis=1)
        s = s + jnp.transpose(bias, (0, 2, 1, 3))[None]
        p = jax.nn.softmax(s, axis=(-2, -1)).astype(v_win.dtype)
        return jnp.einsum("bhwij,biwjhd->bwhd", p, v_win)

    q_rows = jnp.moveaxis(qg, 1, 0)
    out = lax.map(one_row, (jnp.arange(rows, dtype=jnp.int32), q_rows))
    return jnp.moveaxis(out, 0, 1).reshape(B, S, NA_WIDTH)


def swiglu(h, w_gate, w_up, w_down):
    return (jax.nn.silu(h @ w_gate) * (h @ w_up)) @ w_down


def setup_inputs(seed: int = 0) -> dict:
    key = jax.random.key(seed)
    ks = jax.random.split(key, 17)
    L = DEPTH
    in_cols = 2 * GM_WIDTH + 3 * NA_WIDTH + 2 * D_MODEL

    def nrm(k, shape, scale):
        return jax.random.normal(k, shape, jnp.float32) * scale

    return {
        "x": nrm(ks[0], (BATCH, SEQ, D_MODEL), 1.0),
        "norm1_g": 1.0 + nrm(ks[1], (L, D_MODEL), 0.02),
        "w_in": nrm(ks[2], (L, D_MODEL, in_cols), D_MODEL ** -0.5),
        "gm_ln_g": 1.0 + nrm(ks[3], (L, GM_WIDTH), 0.02),
        "gm_ln_b": nrm(ks[4], (L, GM_WIDTH), 0.02),
        "gm_w_s": nrm(ks[5], (L, GM_GROUPS, CHUNK, CHUNK), CHUNK ** -0.5),
        "gm_b_s": 1.0 + nrm(ks[6], (L, GM_GROUPS, CHUNK), 0.02),
        "q_gain": 1.0 + nrm(ks[7], (L, NA_HEAD_DIM), 0.02),
        "k_gain": 1.0 + nrm(ks[8], (L, NA_HEAD_DIM), 0.02),
        "na_rpb": nrm(ks[9], (L, NA_HEADS, 2 * NA_WIN_H_MAX - 1, 2 * NA_WIN_W - 1), 0.02),
        "w_o_gm": nrm(ks[10], (L, GM_WIDTH, D_MODEL), GM_WIDTH ** -0.5),
        "w_o_na": nrm(ks[11], (L, NA_WIDTH, D_MODEL), NA_WIDTH ** -0.5),
        "w_out": nrm(ks[12], (L, D_MODEL, D_MODEL), D_MODEL ** -0.5),
        "norm2_g": 1.0 + nrm(ks[13], (L, D_MODEL), 0.02),
        "w_ff_gate": nrm(ks[14], (L, D_MODEL, D_FF), D_MODEL ** -0.5),
        "w_ff_up": nrm(ks[15], (L, D_MODEL, D_FF), D_MODEL ** -0.5),
        "w_ff_down": nrm(ks[16], (L, D_FF, D_MODEL), D_FF ** -0.5),
    }


def reference(x, norm1_g, w_in, gm_ln_g, gm_ln_b, gm_w_s, gm_b_s, q_gain, k_gain, na_rpb,
              w_o_gm, w_o_na, w_out, norm2_g, w_ff_gate, w_ff_up, w_ff_down):
    split_at = list(np.cumsum([GM_WIDTH, GM_WIDTH, NA_WIDTH, NA_WIDTH, NA_WIDTH, D_MODEL]))
    h = x
    for layer in range(DEPTH):
        xn = rms_norm(h, norm1_g[layer])
        proj = xn @ w_in[layer]
        u_a, v_a, q, k, v, g_a, g_b = jnp.split(proj, split_at, axis=-1)
        y_a = gmlp_spatial_gating(u_a, v_a, gm_ln_g[layer], gm_ln_b[layer],
                                  gm_w_s[layer], gm_b_s[layer])
        y_b = neighbourhood_attention(q, k, v, q_gain[layer], k_gain[layer], na_rpb[layer])
        merged = (jax.nn.sigmoid(g_a) * (y_a @ w_o_gm[layer])
                  + jax.nn.sigmoid(g_b) * (y_b @ w_o_na[layer]))
        h = h + merged @ w_out[layer]
        h = h + swiglu(rms_norm(h, norm2_g[layer]), w_ff_gate[layer], w_ff_up[layer], w_ff_down[layer])
    return h
```

```python
import functools

import numpy as np
import jax
import jax.numpy as jnp
from jax import lax
from jax.experimental import pallas as pl
from jax.experimental.pallas import tpu as pltpu

GRID_W = 64
CHUNK = 128
GM_GROUP_DIM = 128
NA_HEAD_DIM = 128
NA_WIN_H_MAX = 8
NA_WIN_W = 16
RMS_EPS = 1e-6
LN_EPS = 1e-5

V7X_VMEM_BYTES = 64 * 1024 * 1024
VMEM_LIMIT_BYTES = V7X_VMEM_BYTES - 6 * 1024 * 1024

NA_Q_ROWS = 4
NA_K_ROWS = NA_Q_ROWS + NA_WIN_H_MAX
MASK_VALUE = -1e30

F32 = jnp.float32
BF16 = jnp.bfloat16


def _params(*sem):
    return pltpu.CompilerParams(dimension_semantics=sem, vmem_limit_bytes=VMEM_LIMIT_BYTES)


def _pick(n, prefs):
    for p in prefs:
        if n % p == 0:
            return p
    return n


def _rmsnorm_kernel(x_ref, g_ref, o_ref):
    x = x_ref[...]
    ms = jnp.mean(x * x, axis=-1, keepdims=True)
    o_ref[...] = (x * lax.rsqrt(ms + RMS_EPS) * g_ref[...]).astype(o_ref.dtype)


def _rmsnorm(x, g):
    s, d = x.shape
    tr = _pick(s, (512, 256, 128))
    return pl.pallas_call(
        _rmsnorm_kernel,
        grid=(s // tr,),
        in_specs=[pl.BlockSpec((tr, d), lambda i: (i, 0)),
                  pl.BlockSpec((1, d), lambda i: (0, 0))],
        out_specs=pl.BlockSpec((tr, d), lambda i: (i, 0)),
        out_shape=jax.ShapeDtypeStruct((s, d), BF16),
        compiler_params=_params("parallel"),
        name="rmsnorm",
    )(x, g.reshape(1, d))


def _gelu(x):
    return 0.5 * x * (1.0 + lax.erf(x * np.float32(np.sqrt(0.5))))


def _sigmoid(x):
    return 1.0 / (1.0 + jnp.exp(-x))


def _proj_kernel(x_ref, w_ref, *rest, epilogue):
    o_ref = rest[-1]
    acc = jnp.dot(x_ref[...], w_ref[...], preferred_element_type=F32)
    if epilogue == "gelu":
        out = _gelu(acc)
    elif epilogue == "sigmoid":
        out = _sigmoid(acc)
    elif epilogue == "headnorm":
        gain_ref = rest[0]
        parts = []
        for h in range(acc.shape[1] // NA_HEAD_DIM):
            blk = acc[:, h * NA_HEAD_DIM:(h + 1) * NA_HEAD_DIM]
            ms = jnp.mean(blk * blk, axis=-1, keepdims=True)
            parts.append(blk * lax.rsqrt(ms + RMS_EPS))
        out = jnp.concatenate(parts, axis=1) * gain_ref[...]
    else:
        out = acc
    o_ref[...] = out.astype(o_ref.dtype)


def _project(xn, w, col_start, n_cols, epilogue, gain=None):
    s, d = xn.shape
    tm = _pick(s, (1024, 512, 256, 128))
    tn = _pick(int(np.gcd(n_cols, col_start)), (1024, 512, 256, 128))
    off = col_start // tn
    in_specs = [pl.BlockSpec((tm, d), lambda i, j: (i, 0)),
                pl.BlockSpec((d, tn), lambda i, j: (0, j + off))]
    args = [xn, w]
    if gain is not None:
        in_specs.append(pl.BlockSpec((1, tn), lambda i, j: (0, j)))
        args.append(gain)
    return pl.pallas_call(
        functools.partial(_proj_kernel, epilogue=epilogue),
        grid=(s // tm, n_cols // tn),
        in_specs=in_specs,
        out_specs=pl.BlockSpec((tm, tn), lambda i, j: (i, j)),
        out_shape=jax.ShapeDtypeStruct((s, n_cols), BF16),
        compiler_params=_params("parallel", "parallel"),
        name="proj_" + epilogue,
    )(*args)


def _gmlp_kernel(u_ref, v_ref, lng_ref, lnb_ref, ws_ref, bs_ref, o_ref):
    v = v_ref[...].astype(F32)
    mu = jnp.mean(v, axis=-1, keepdims=True)
    xc = v - mu
    var = jnp.mean(xc * xc, axis=-1, keepdims=True)
    vn = (xc * lax.rsqrt(var + LN_EPS) * lng_ref[...] + lnb_ref[...]).astype(BF16)
    t, width = vn.shape
    for c in range(t // CHUNK):
        rows = slice(c * CHUNK, (c + 1) * CHUNK)
        for g in range(width // GM_GROUP_DIM):
            cols = slice(g * GM_GROUP_DIM, (g + 1) * GM_GROUP_DIM)
            mixed = jnp.dot(ws_ref[g], vn[rows, cols], preferred_element_type=F32) + bs_ref[:, cols]
            o_ref[rows, cols] = (u_ref[rows, cols].astype(F32) * mixed).astype(o_ref.dtype)


def _gmlp(uv, ln_g, ln_b, w_s, b_s):
    s = uv.shape[0]
    width = uv.shape[1] // 2
    groups = width // GM_GROUP_DIM
    t = _pick(s, (512, 256, 128))
    bs_full = jnp.repeat(b_s.T, GM_GROUP_DIM, axis=1)
    return pl.pallas_call(
        _gmlp_kernel,
        grid=(s // t,),
        in_specs=[pl.BlockSpec((t, width), lambda i: (i, 0)),
                  pl.BlockSpec((t, width), lambda i: (i, 1)),
                  pl.BlockSpec((1, width), lambda i: (0, 0)),
                  pl.BlockSpec((1, width), lambda i: (0, 0)),
                  pl.BlockSpec((groups, CHUNK, CHUNK), lambda i: (0, 0, 0)),
                  pl.BlockSpec((CHUNK, width), lambda i: (0, 0))],
        out_specs=pl.BlockSpec((t, width), lambda i: (i, 0)),
        out_shape=jax.ShapeDtypeStruct((s, width), BF16),
        compiler_params=_params("parallel"),
        name="gmlp",
    )(uv, uv, ln_g.reshape(1, width), ln_b.reshape(1, width), w_s.astype(BF16), bs_full)


def _na_bias_tables(rows):
    kh = min(NA_WIN_H_MAX, rows)
    assert kh == NA_WIN_H_MAX and rows % NA_Q_ROWS == 0 and rows >= 2 * NA_K_ROWS
    n_rel_w = 2 * NA_WIN_W - 1
    idx = np.zeros((3, NA_Q_ROWS * GRID_W, NA_K_ROWS * GRID_W), np.int32)
    mask = np.zeros(idx.shape, bool)
    starts = ((0, 0), (NA_Q_ROWS, 0), (rows - NA_Q_ROWS, rows - NA_K_ROWS))
    c = np.arange(GRID_W)
    cs = np.clip(c - NA_WIN_W // 2, 0, GRID_W - NA_WIN_W)
    for var, (r0, kb0) in enumerate(starts):
        r = r0 + np.repeat(np.arange(NA_Q_ROWS), GRID_W)
        qc = np.tile(c, NA_Q_ROWS)
        qcs = np.tile(cs, NA_Q_ROWS)
        kr = kb0 + np.repeat(np.arange(NA_K_ROWS), GRID_W)
        kc = np.tile(c, NA_K_ROWS)
        rs = np.clip(r - kh // 2, 0, rows - kh)
        row_ok = (kr[None, :] >= rs[:, None]) & (kr[None, :] < rs[:, None] + kh)
        col_ok = (kc[None, :] >= qcs[:, None]) & (kc[None, :] < qcs[:, None] + NA_WIN_W)
        a = kr[None, :] - r[:, None] + (NA_WIN_H_MAX - 1)
        b = kc[None, :] - qc[:, None] + (NA_WIN_W - 1)
        ok = row_ok & col_ok
        mask[var] = ok
        idx[var] = np.where(ok, a * n_rel_w + b, 0)
    return idx, mask


def _na_kernel(q_ref, k_ref, v_ref, bias_ref, o_ref, *, n_blocks):
    tq = NA_Q_ROWS * GRID_W
    tk = NA_K_ROWS * GRID_W
    scale = np.float32(NA_HEAD_DIM ** -0.5)

    def body(qb, carry):
        q0 = pl.multiple_of(qb * tq, tq)
        k0 = pl.multiple_of(jnp.clip(qb - 1, 0, n_blocks - NA_K_ROWS // NA_Q_ROWS) * tq, tq)
        variant = (qb > 0).astype(jnp.int32) + (qb == n_blocks - 1).astype(jnp.int32)
        q = q_ref[pl.ds(q0, tq), :]
        k = k_ref[pl.ds(k0, tk), :]
        v = v_ref[pl.ds(k0, tk), :]
        s = lax.dot_general(q, k, (((1,), (1,)), ((), ())), preferred_element_type=F32)
        s = s * scale + bias_ref[variant]
        m = jnp.max(s, axis=-1, keepdims=True)
        p = jnp.exp(s - m)
        l = jnp.sum(p, axis=-1, keepdims=True)
        o = jnp.dot(p.astype(BF16), v, preferred_element_type=F32) / l
        o_ref[pl.ds(q0, tq), :] = o.astype(o_ref.dtype)
        return carry

    lax.fori_loop(0, n_blocks, body, 0)


def _neighbourhood_attention(qk, v, rpb):
    s = v.shape[0]
    width = v.shape[1]
    heads = width // NA_HEAD_DIM
    rows = s // GRID_W
    n_blocks = rows // NA_Q_ROWS
    idx, mask = _na_bias_tables(rows)
    table = rpb.astype(F32).reshape(heads, -1)
    bias = jnp.where(mask[None], table[:, idx], np.float32(MASK_VALUE))
    tq, tk = idx.shape[1:]
    return pl.pallas_call(
        functools.partial(_na_kernel, n_blocks=n_blocks),
        grid=(heads,),
        in_specs=[pl.BlockSpec((s, NA_HEAD_DIM), lambda h: (0, h)),
                  pl.BlockSpec((s, NA_HEAD_DIM), lambda h: (0, heads + h)),
                  pl.BlockSpec((s, NA_HEAD_DIM), lambda h: (0, h)),
                  pl.BlockSpec((None, 3, tq, tk), lambda h: (h, 0, 0, 0))],
        out_specs=pl.BlockSpec((s, NA_HEAD_DIM), lambda h: (0, h)),
        out_shape=jax.ShapeDtypeStruct((s, width), BF16),
        compiler_params=_params("parallel"),
        name="natten",
    )(qk, qk, v, bias)


def _merge_kernel(ya_ref, yb_ref, wa_ref, wb_ref, ga_ref, gb_ref, o_ref):
    a = jnp.dot(ya_ref[...], wa_ref[...], preferred_element_type=F32)
    b = jnp.dot(yb_ref[...], wb_ref[...], preferred_element_type=F32)
    o_ref[...] = (ga_ref[...].astype(F32) * a + gb_ref[...].astype(F32) * b).astype(o_ref.dtype)


def _merge(y_a, y_b, w_oa, w_ob, sg):
    s, k = y_a.shape
    n = w_oa.shape[1]
    tm = _pick(s, (1024, 512, 256, 128))
    tn = _pick(n, (1024, 512, 256, 128))
    gb_off = n // tn
    return pl.pallas_call(
        _merge_kernel,
        grid=(s // tm, n // tn),
        in_specs=[pl.BlockSpec((tm, k), lambda i, j: (i, 0)),
                  pl.BlockSpec((tm, k), lambda i, j: (i, 0)),
                  pl.BlockSpec((k, tn), lambda i, j: (0, j)),
                  pl.BlockSpec((k, tn), lambda i, j: (0, j)),
                  pl.BlockSpec((tm, tn), lambda i, j: (i, j)),
                  pl.BlockSpec((tm, tn), lambda i, j: (i, j + gb_off))],
        out_specs=pl.BlockSpec((tm, tn), lambda i, j: (i, j)),
        out_shape=jax.ShapeDtypeStruct((s, n), BF16),
        compiler_params=_params("parallel", "parallel"),
        name="merge",
    )(y_a, y_b, w_oa, w_ob, sg, sg)


def _matmul_residual_kernel(x_ref, w_ref, r_ref, o_ref):
    o_ref[...] = r_ref[...] + jnp.dot(x_ref[...], w_ref[...], preferred_element_type=F32)


def _matmul_residual(lhs, w, residual, tm_prefs, tn_prefs, name):
    s, k = lhs.shape
    n = w.shape[1]
    tm = _pick(s, tm_prefs)
    tn = _pick(n, tn_prefs)
    return pl.pallas_call(
        _matmul_residual_kernel,
        grid=(s // tm, n // tn),
        in_specs=[pl.BlockSpec((tm, k), lambda i, j: (i, 0)),
                  pl.BlockSpec((k, tn), lambda i, j: (0, j)),
                  pl.BlockSpec((tm, tn), lambda i, j: (i, j))],
        out_specs=pl.BlockSpec((tm, tn), lambda i, j: (i, j)),
        out_shape=jax.ShapeDtypeStruct((s, n), F32),
        compiler_params=_params("parallel", "parallel"),
        name=name,
    )(lhs, w, residual)


def _swiglu_kernel(x_ref, wg_ref, wu_ref, o_ref):
    x = x_ref[...]
    g = jnp.dot(x, wg_ref[...], preferred_element_type=F32)
    u = jnp.dot(x, wu_ref[...], preferred_element_type=F32)
    o_ref[...] = (g * _sigmoid(g) * u).astype(o_ref.dtype)


def _swiglu_up(hn, w_gate, w_up):
    s, d = hn.shape
    f = w_gate.shape[1]
    tm = _pick(s, (2048, 1024, 512, 256, 128))
    tn = _pick(f, (256, 128))
    return pl.pallas_call(
        _swiglu_kernel,
        grid=(s // tm, f // tn),
        in_specs=[pl.BlockSpec((tm, d), lambda i, j: (i, 0)),
                  pl.BlockSpec((d, tn), lambda i, j: (0, j)),
                  pl.BlockSpec((d, tn), lambda i, j: (0, j))],
        out_specs=pl.BlockSpec((tm, tn), lambda i, j: (i, j)),
        out_shape=jax.ShapeDtypeStruct((s, f), BF16),
        compiler_params=_params("parallel", "parallel"),
        name="swiglu_up",
    )(hn, w_gate, w_up)


def kernel(x, norm1_g, w_in, gm_ln_g, gm_ln_b, gm_w_s, gm_b_s, q_gain, k_gain, na_rpb,
           w_o_gm, w_o_na, w_out, norm2_g, w_ff_gate, w_ff_up, w_ff_down):
    batch, seq, d_model = x.shape
    depth = w_in.shape[0]
    gm_width = gm_ln_g.shape[1]
    na_width = w_o_na.shape[1]
    heads = na_width // NA_HEAD_DIM
    outs = []
    for bi in range(batch):
        h = x.reshape(seq, d_model) if batch == 1 else x[bi]
        for layer in range(depth):
            w_in_l = w_in[layer].astype(BF16)
            xn = _rmsnorm(h, norm1_g[layer])
            uv = _project(xn, w_in_l, 0, 2 * gm_width, "gelu")
            gains = jnp.concatenate([jnp.tile(q_gain[layer], heads),
                                     jnp.tile(k_gain[layer], heads)]).reshape(1, 2 * na_width)
            qk = _project(xn, w_in_l, 2 * gm_width, 2 * na_width, "headnorm", gains)
            v = _project(xn, w_in_l, 2 * gm_width + 2 * na_width, na_width, "none")
            sg = _project(xn, w_in_l, 2 * gm_width + 3 * na_width, 2 * d_model, "sigmoid")
            y_a = _gmlp(uv, gm_ln_g[layer], gm_ln_b[layer], gm_w_s[layer], gm_b_s[layer])
            y_b = _neighbourhood_attention(qk, v, na_rpb[layer])
            merged = _merge(y_a, y_b, w_o_gm[layer].astype(BF16), w_o_na[layer].astype(BF16), sg)
            h = _matmul_residual(merged, w_out[layer].astype(BF16), h,
                                 (1024, 512, 256, 128), (1024, 512, 256, 128), "out_proj")
            hn = _rmsnorm(h, norm2_g[layer])
            act = _swiglu_up(hn, w_ff_gate[layer].astype(BF16), w_ff_up[layer].astype(BF16))
            h = _matmul_residual(act, w_ff_down[layer].astype(BF16), h,
                                 (512, 256, 128), (512, 256, 128), "ffn_down")
        outs.append(h)
    if batch == 1:
        return outs[0].reshape(1, seq, d_model)
    return jnp.stack(outs)
```

```python
import functools

import numpy as np
import jax
import jax.numpy as jnp
from jax import lax
from jax.experimental import pallas as pl
from jax.experimental.pallas import tpu as pltpu

GRID_W = 64
CHUNK = 128
GM_GROUP_DIM = 128
NA_HEAD_DIM = 128
NA_WIN_H_MAX = 8
NA_WIN_W = 16
RMS_EPS = 1e-6
LN_EPS = 1e-5

V7X_VMEM_BYTES = 64 * 1024 * 1024
VMEM_LIMIT_BYTES = V7X_VMEM_BYTES - 6 * 1024 * 1024

NA_Q_ROWS = 4
NA_K_ROWS = NA_Q_ROWS + NA_WIN_H_MAX
MASK_VALUE = -1e30

F32 = jnp.float32
BF16 = jnp.bfloat16


def _params(*sem):
    return pltpu.CompilerParams(dimension_semantics=sem, vmem_limit_bytes=VMEM_LIMIT_BYTES)


def _pick(n, prefs):
    for p in prefs:
        if n % p == 0:
            return p
    return n


def _rmsnorm_kernel(x_ref, g_ref, o_ref):
    x = x_ref[...]
    ms = jnp.mean(x * x, axis=-1, keepdims=True)
    o_ref[...] = (x * lax.rsqrt(ms + RMS_EPS) * g_ref[...]).astype(o_ref.dtype)


def _rmsnorm(x, g):
    s, d = x.shape
    tr = _pick(s, (512, 256, 128))
    return pl.pallas_call(
        _rmsnorm_kernel,
        grid=(s // tr,),
        in_specs=[pl.BlockSpec((tr, d), lambda i: (i, 0)),
                  pl.BlockSpec((1, d), lambda i: (0, 0))],
        out_specs=pl.BlockSpec((tr, d), lambda i: (i, 0)),
        out_shape=jax.ShapeDtypeStruct((s, d), BF16),
        compiler_params=_params("parallel"),
        name="rmsnorm",
    )(x, g.reshape(1, d))


def _gelu(x):
    return 0.5 * x * (1.0 + lax.erf(x * np.float32(np.sqrt(0.5))))


def _sigmoid(x):
    return 0.5 * (jnp.tanh(0.5 * x) + 1.0)


BF16_SUBLANES = 16


def _cast_side(refs, n_side):
    if n_side:
        for src, dst in zip(refs[-2 * n_side - 1:-n_side - 1], refs[-n_side:]):
            dst[...] = src[...].astype(dst.dtype)


def _side_specs(side, gi, gj):
    steps = gi * gj
    in_specs, out_specs, out_shapes = [], [], []
    for w in side:
        rows = w.shape[0] // steps
        assert rows * steps == w.shape[0] and rows % BF16_SUBLANES == 0
        spec = pl.BlockSpec((rows, w.shape[1]), lambda i, j: (i * gj + j, 0))
        in_specs.append(spec)
        out_specs.append(spec)
        out_shapes.append(jax.ShapeDtypeStruct(w.shape, BF16))
    return in_specs, out_specs, out_shapes


def _split_side(side, steps):
    rides = [w.shape[0] % (steps * BF16_SUBLANES) == 0 for w in side]
    return [w for w, r in zip(side, rides) if r], rides


def _merge_side(side, rides, cast):
    cast = list(cast)
    return [cast.pop(0) if r else w.astype(BF16) for w, r in zip(side, rides)]


def _proj_kernel(x_ref, w_ref, *rest, epilogue, n_side):
    o_ref = rest[-n_side - 1]
    acc = jnp.dot(x_ref[...], w_ref[...], preferred_element_type=F32)
    if epilogue == "gelu":
        out = _gelu(acc)
    elif epilogue == "sigmoid":
        out = _sigmoid(acc)
    elif epilogue == "headnorm":
        gain_ref = rest[0]
        parts = []
        for h in range(acc.shape[1] // NA_HEAD_DIM):
            blk = acc[:, h * NA_HEAD_DIM:(h + 1) * NA_HEAD_DIM]
            ms = jnp.mean(blk * blk, axis=-1, keepdims=True)
            parts.append(blk * lax.rsqrt(ms + RMS_EPS))
        out = jnp.concatenate(parts, axis=1) * gain_ref[...]
    else:
        out = acc
    o_ref[...] = out.astype(o_ref.dtype)
    _cast_side(rest, n_side)


def _project(xn, w, col_start, n_cols, epilogue, gain=None, side=()):
    s, d = xn.shape
    tm = _pick(s, (1024, 512, 256, 128))
    tn = _pick(int(np.gcd(n_cols, col_start)), (1024, 512, 256, 128))
    off = col_start // tn
    gi, gj = s // tm, n_cols // tn
    in_specs = [pl.BlockSpec((tm, d), lambda i, j: (i, 0)),
                pl.BlockSpec((d, tn), lambda i, j: (0, j + off))]
    args = [xn, w]
    if gain is not None:
        in_specs.append(pl.BlockSpec((1, tn), lambda i, j: (0, j)))
        args.append(gain)
    riding, rides = _split_side(side, gi * gj)
    side_in, side_out, side_shapes = _side_specs(riding, gi, gj)
    outs = pl.pallas_call(
        functools.partial(_proj_kernel, epilogue=epilogue, n_side=len(riding)),
        grid=(gi, gj),
        in_specs=in_specs + side_in,
        out_specs=[pl.BlockSpec((tm, tn), lambda i, j: (i, j))] + side_out,
        out_shape=[jax.ShapeDtypeStruct((s, n_cols), BF16)] + side_shapes,
        compiler_params=_params("parallel", "parallel"),
        name="proj_" + epilogue,
    )(*args, *riding)
    return outs[0], _merge_side(side, rides, outs[1:])


def _gmlp_kernel(u_ref, v_ref, lng_ref, lnb_ref, ws_ref, bs_ref, o_ref):
    v = v_ref[...].astype(F32)
    mu = jnp.mean(v, axis=-1, keepdims=True)
    xc = v - mu
    var = jnp.mean(xc * xc, axis=-1, keepdims=True)
    vn = (xc * lax.rsqrt(var + LN_EPS) * lng_ref[...] + lnb_ref[...]).astype(BF16)
    t, width = vn.shape
    for c in range(t // CHUNK):
        rows = slice(c * CHUNK, (c + 1) * CHUNK)
        for g in range(width // GM_GROUP_DIM):
            cols = slice(g * GM_GROUP_DIM, (g + 1) * GM_GROUP_DIM)
            mixed = jnp.dot(ws_ref[g], vn[rows, cols], preferred_element_type=F32) + bs_ref[:, cols]
            o_ref[rows, cols] = (u_ref[rows, cols].astype(F32) * mixed).astype(o_ref.dtype)


def _gmlp(uv, ln_g, ln_b, w_s, b_s):
    s = uv.shape[0]
    width = uv.shape[1] // 2
    groups = width // GM_GROUP_DIM
    t = _pick(s, (512, 256, 128))
    bs_full = jnp.repeat(b_s.T, GM_GROUP_DIM, axis=1)
    return pl.pallas_call(
        _gmlp_kernel,
        grid=(s // t,),
        in_specs=[pl.BlockSpec((t, width), lambda i: (i, 0)),
                  pl.BlockSpec((t, width), lambda i: (i, 1)),
                  pl.BlockSpec((1, width), lambda i: (0, 0)),
                  pl.BlockSpec((1, width), lambda i: (0, 0)),
                  pl.BlockSpec((groups, CHUNK, CHUNK), lambda i: (0, 0, 0)),
                  pl.BlockSpec((CHUNK, width), lambda i: (0, 0))],
        out_specs=pl.BlockSpec((t, width), lambda i: (i, 0)),
        out_shape=jax.ShapeDtypeStruct((s, width), BF16),
        compiler_params=_params("parallel"),
        name="gmlp",
    )(uv, uv, ln_g.reshape(1, width), ln_b.reshape(1, width), w_s.astype(BF16), bs_full)


NA_MASKED_OFFSET = 2 * NA_WIN_H_MAX - 1
NA_PAIR = 2
NA_GROUP = 4


def _na_strip_plan(rows):
    kh = min(NA_WIN_H_MAX, rows)
    assert kh == NA_WIN_H_MAX and rows % NA_Q_ROWS == 0 and rows >= 2 * NA_K_ROWS
    assert NA_PAIR * GRID_W == 128 and NA_K_ROWS % NA_PAIR == 0
    starts = ((0, 0), (NA_Q_ROWS, 0), (rows - NA_Q_ROWS, rows - NA_K_ROWS))
    pairs, plan = [], []
    for r0, kb0 in starts:
        strips = []
        for rl in range(NA_Q_ROWS):
            r = r0 + rl
            rs = int(np.clip(r - kh // 2, 0, rows - kh))
            offs = [kb0 + krl - r + (NA_WIN_H_MAX - 1) if rs <= kb0 + krl < rs + kh else NA_MASKED_OFFSET
                    for krl in range(NA_K_ROWS)]
            row_pairs = [tuple(offs[j:j + NA_PAIR]) for j in range(0, NA_K_ROWS, NA_PAIR)]
            live = [j for j, pr in enumerate(row_pairs) if any(o != NA_MASKED_OFFSET for o in pr)]
            lo, hi = live[0], live[-1] + 1
            ids = []
            for pr in row_pairs[lo:hi]:
                if pr not in pairs:
                    pairs.append(pr)
                ids.append(pairs.index(pr))
            strips.append((lo, hi, tuple(ids)))
        plan.append(tuple(strips))
    return tuple(plan), np.asarray(pairs, np.int32)


def _na_offsets(qb, n_blocks):
    tq = NA_Q_ROWS * GRID_W
    kb = jnp.clip(qb - 1, 0, n_blocks - NA_K_ROWS // NA_Q_ROWS)
    return pl.multiple_of(qb * tq, tq), pl.multiple_of(kb * tq, tq)


def _na_scores(q_ref, k_ref, s_ref, qb, n_blocks):
    q0, k0 = _na_offsets(qb, n_blocks)
    q = q_ref[pl.ds(q0, NA_Q_ROWS * GRID_W), :]
    k = k_ref[pl.ds(k0, NA_K_ROWS * GRID_W), :]
    s_ref[...] = lax.dot_general(q, k, (((1,), (1,)), ((), ())), preferred_element_type=F32)


def _na_softmax_pv(s_ref, v_ref, pair_ref, o_ref, qb, n_blocks, strips):
    tq = NA_Q_ROWS * GRID_W
    tk = NA_K_ROWS * GRID_W
    lanes = NA_PAIR * GRID_W
    q0, k0 = _na_offsets(qb, n_blocks)
    v = v_ref[pl.ds(k0, tk), :]
    p_rows, l_rows = [], []
    for rl, (lo, hi, ids) in enumerate(strips):
        bias = jnp.concatenate([pair_ref[i] for i in ids], axis=1)
        s_rl = s_ref[rl * GRID_W:(rl + 1) * GRID_W, lo * lanes:hi * lanes] + bias
        m = jnp.max(s_rl, axis=-1, keepdims=True)
        p = jnp.exp(s_rl - m)
        l_rows.append(jnp.sum(p, axis=-1, keepdims=True))
        pieces = [p.astype(BF16)]
        if lo > 0:
            pieces.insert(0, jnp.zeros((GRID_W, lo * lanes), BF16))
        if hi * lanes < tk:
            pieces.append(jnp.zeros((GRID_W, tk - hi * lanes), BF16))
        p_rows.append(jnp.concatenate(pieces, axis=1))
    p = jnp.concatenate(p_rows, axis=0)
    o = jnp.dot(p, v, preferred_element_type=F32) / jnp.concatenate(l_rows, axis=0)
    o_ref[pl.ds(q0, tq), :] = o.astype(o_ref.dtype)


def _na_kernel(q_ref, k_ref, v_ref, pair_ref, o_ref, s_even, s_odd, *, n_blocks, group, plan):
    first, interior, last = plan
    n_groups = n_blocks // group
    bufs = (s_even, s_odd)

    def scores(g, buf):
        for u in range(group):
            _na_scores(q_ref, k_ref, buf.at[u], g * group + u, n_blocks)

    def finish(g, buf, strips):
        for u in range(group):
            _na_softmax_pv(buf.at[u], v_ref, pair_ref, o_ref, g * group + u, n_blocks, strips[u])

    def strips_of(g):
        strips = [interior] * group
        if g == 0:
            strips[0] = first
        if g == n_groups - 1:
            strips[-1] = last
        return strips

    def step(g, parity, strips):
        finish(g, bufs[parity], strips)
        scores(g + 1, bufs[1 - parity])

    scores(0, s_even)
    step(0, 0, strips_of(0))
    n_mid = n_groups - 2

    def body(t, carry):
        g = 1 + 2 * t
        step(g, 1, [interior] * group)
        step(g + 1, 0, [interior] * group)
        return carry

    lax.fori_loop(0, n_mid // 2, body, 0)
    if n_mid % 2:
        step(n_groups - 2, (n_groups - 2) % 2, strips_of(n_groups - 2))
    finish(n_groups - 1, bufs[(n_groups - 1) % 2], strips_of(n_groups - 1))


def _neighbourhood_attention(qk, v, rpb):
    s = v.shape[0]
    width = v.shape[1]
    heads = width // NA_HEAD_DIM
    rows = s // GRID_W
    n_blocks = rows // NA_Q_ROWS
    plan, pairs = _na_strip_plan(rows)
    c = np.arange(GRID_W)
    cs = np.clip(c - NA_WIN_W // 2, 0, GRID_W - NA_WIN_W)
    col_ok = (c[None, :] >= cs[:, None]) & (c[None, :] < cs[:, None] + NA_WIN_W)
    rel = np.clip(c[None, :] - c[:, None] + (NA_WIN_W - 1), 0, 2 * NA_WIN_W - 2)
    tiles = jnp.where(col_ok[None, None], rpb.astype(F32)[:, :, rel], np.float32(MASK_VALUE))
    tiles = jnp.concatenate(
        [tiles, jnp.full((heads, 1, GRID_W, GRID_W), MASK_VALUE, F32)], axis=1)
    pair_tiles = jnp.concatenate([tiles[:, pairs[:, j]] for j in range(NA_PAIR)], axis=-1)
    n_pairs = pairs.shape[0]
    group = _pick(n_blocks // 2, (NA_GROUP, 2, 1))
    s_buf = pltpu.VMEM((group, NA_Q_ROWS * GRID_W, NA_K_ROWS * GRID_W), F32)
    return pl.pallas_call(
        functools.partial(_na_kernel, n_blocks=n_blocks, group=group, plan=plan),
        grid=(heads,),
        scratch_shapes=[s_buf, s_buf],
        in_specs=[pl.BlockSpec((s, NA_HEAD_DIM), lambda h: (0, h)),
                  pl.BlockSpec((s, NA_HEAD_DIM), lambda h: (0, heads + h)),
                  pl.BlockSpec((s, NA_HEAD_DIM), lambda h: (0, h)),
                  pl.BlockSpec((None, n_pairs, GRID_W, NA_PAIR * GRID_W), lambda h: (h, 0, 0, 0))],
        out_specs=pl.BlockSpec((s, NA_HEAD_DIM), lambda h: (0, h)),
        out_shape=jax.ShapeDtypeStruct((s, width), BF16),
        compiler_params=_params("parallel"),
        name="natten",
    )(qk, qk, v, pair_tiles)


def _merge_kernel(ya_ref, yb_ref, wa_ref, wb_ref, ga_ref, gb_ref, o_ref):
    a = jnp.dot(ya_ref[...], wa_ref[...], preferred_element_type=F32)
    b = jnp.dot(yb_ref[...], wb_ref[...], preferred_element_type=F32)
    o_ref[...] = (ga_ref[...].astype(F32) * a + gb_ref[...].astype(F32) * b).astype(o_ref.dtype)


def _merge(y_a, y_b, w_oa, w_ob, sg):
    s, k = y_a.shape
    n = w_oa.shape[1]
    tm = _pick(s, (1024, 512, 256, 128))
    tn = _pick(n, (1024, 512, 256, 128))
    gb_off = n // tn
    return pl.pallas_call(
        _merge_kernel,
        grid=(s // tm, n // tn),
        in_specs=[pl.BlockSpec((tm, k), lambda i, j: (i, 0)),
                  pl.BlockSpec((tm, k), lambda i, j: (i, 0)),
                  pl.BlockSpec((k, tn), lambda i, j: (0, j)),
                  pl.BlockSpec((k, tn), lambda i, j: (0, j)),
                  pl.BlockSpec((tm, tn), lambda i, j: (i, j)),
                  pl.BlockSpec((tm, tn), lambda i, j: (i, j + gb_off))],
        out_specs=pl.BlockSpec((tm, tn), lambda i, j: (i, j)),
        out_shape=jax.ShapeDtypeStruct((s, n), BF16),
        compiler_params=_params("parallel", "parallel"),
        name="merge",
    )(y_a, y_b, w_oa, w_ob, sg, sg)


def _matmul_residual_kernel(x_ref, w_ref, r_ref, o_ref):
    o_ref[...] = r_ref[...] + jnp.dot(x_ref[...], w_ref[...], preferred_element_type=F32)


def _matmul_residual(lhs, w, residual, tm_prefs, tn_prefs, name):
    s, k = lhs.shape
    n = w.shape[1]
    tm = _pick(s, tm_prefs)
    tn = _pick(n, tn_prefs)
    return pl.pallas_call(
        _matmul_residual_kernel,
        grid=(s // tm, n // tn),
        in_specs=[pl.BlockSpec((tm, k), lambda i, j: (i, 0)),
                  pl.BlockSpec((k, tn), lambda i, j: (0, j)),
                  pl.BlockSpec((tm, tn), lambda i, j: (i, j))],
        out_specs=pl.BlockSpec((tm, tn), lambda i, j: (i, j)),
        out_shape=jax.ShapeDtypeStruct((s, n), F32),
        compiler_params=_params("parallel", "parallel"),
        name=name,
    )(lhs, w, residual)


def _swiglu_kernel(x_ref, wg_ref, wu_ref, *rest, n_side):
    o_ref = rest[-n_side - 1]
    x = x_ref[...]
    g = jnp.dot(x, wg_ref[...], preferred_element_type=F32)
    u = jnp.dot(x, wu_ref[...], preferred_element_type=F32)
    o_ref[...] = (g * _sigmoid(g) * u).astype(o_ref.dtype)
    _cast_side(rest, n_side)


def _swiglu_up(hn, w_gate, w_up, side=()):
    s, d = hn.shape
    f = w_gate.shape[1]
    tm = _pick(s, (2048, 1024, 512, 256, 128))
    tn = _pick(f, (256, 128))
    gi, gj = s // tm, f // tn
    riding, rides = _split_side(side, gi * gj)
    side_in, side_out, side_shapes = _side_specs(riding, gi, gj)
    outs = pl.pallas_call(
        functools.partial(_swiglu_kernel, n_side=len(riding)),
        grid=(gi, gj),
        in_specs=[pl.BlockSpec((tm, d), lambda i, j: (i, 0)),
                  pl.BlockSpec((d, tn), lambda i, j: (0, j)),
                  pl.BlockSpec((d, tn), lambda i, j: (0, j))] + side_in,
        out_specs=[pl.BlockSpec((tm, tn), lambda i, j: (i, j))] + side_out,
        out_shape=[jax.ShapeDtypeStruct((s, f), BF16)] + side_shapes,
        compiler_params=_params("parallel", "parallel"),
        name="swiglu_up",
    )(hn, w_gate, w_up, *riding)
    return outs[0], _merge_side(side, rides, outs[1:])


def kernel(x, norm1_g, w_in, gm_ln_g, gm_ln_b, gm_w_s, gm_b_s, q_gain, k_gain, na_rpb,
           w_o_gm, w_o_na, w_out, norm2_g, w_ff_gate, w_ff_up, w_ff_down):
    batch, seq, d_model = x.shape
    depth = w_in.shape[0]
    gm_width = gm_ln_g.shape[1]
    na_width = w_o_na.shape[1]
    heads = na_width // NA_HEAD_DIM
    outs = []
    for bi in range(batch):
        h = x.reshape(seq, d_model) if batch == 1 else x[bi]
        for layer in range(depth):
            w_in_l = w_in[layer].astype(BF16)
            xn = _rmsnorm(h, norm1_g[layer])
            uv, (w_oa, w_ob, w_o) = _project(xn, w_in_l, 0, 2 * gm_width, "gelu",
                                             side=(w_o_gm[layer], w_o_na[layer], w_out[layer]))
            gains = jnp.concatenate([jnp.tile(q_gain[layer] * np.float32(NA_HEAD_DIM ** -0.5), heads),
                                     jnp.tile(k_gain[layer], heads)]).reshape(1, 2 * na_width)
            qk, _ = _project(xn, w_in_l, 2 * gm_width, 2 * na_width, "headnorm", gains)
            v, _ = _project(xn, w_in_l, 2 * gm_width + 2 * na_width, na_width, "none")
            sg, (w_gate, w_up) = _project(xn, w_in_l, 2 * gm_width + 3 * na_width, 2 * d_model, "sigmoid",
                                          side=(w_ff_gate[layer], w_ff_up[layer]))
            y_a = _gmlp(uv, gm_ln_g[layer], gm_ln_b[layer], gm_w_s[layer], gm_b_s[layer])
            y_b = _neighbourhood_attention(qk, v, na_rpb[layer])
            merged = _merge(y_a, y_b, w_oa, w_ob, sg)
            h = _matmul_residual(merged, w_o, h,
                                 (1024, 512, 256, 128), (1024, 512, 256, 128), "out_proj")
            hn = _rmsnorm(h, norm2_g[layer])
            act, (w_down,) = _swiglu_up(hn, w_gate, w_up, side=(w_ff_down[layer],))
            h = _matmul_residual(act, w_down, h,
                                 (512, 256, 128), (512, 256, 128), "ffn_down")
        outs.append(h)
    if batch == 1:
        return outs[0].reshape(1, seq, d_model)
    return jnp.stack(outs)
```

```python
import functools

import numpy as np
import jax
import jax.numpy as jnp
from jax import lax
from jax.experimental import pallas as pl
from jax.experimental.pallas import tpu as pltpu

GRID_W = 64
CHUNK = 128
GM_GROUP_DIM = 128
NA_HEAD_DIM = 128
NA_WIN_H_MAX = 8
NA_WIN_W = 16
RMS_EPS = 1e-6
LN_EPS = 1e-5

V7X_VMEM_BYTES = 64 * 1024 * 1024
VMEM_LIMIT_BYTES = V7X_VMEM_BYTES - 6 * 1024 * 1024

NA_Q_ROWS = 4
NA_K_ROWS = NA_Q_ROWS + NA_WIN_H_MAX
MASK_VALUE = -1e30

F32 = jnp.float32
BF16 = jnp.bfloat16


def _params(*sem):
    return pltpu.CompilerParams(dimension_semantics=sem, vmem_limit_bytes=VMEM_LIMIT_BYTES)


def _pick(n, prefs):
    for p in prefs:
        if n % p == 0:
            return p
    return n


def _rmsnorm_kernel(x_ref, g_ref, o_ref):
    x = x_ref[...]
    ms = jnp.mean(x * x, axis=-1, keepdims=True)
    o_ref[...] = (x * lax.rsqrt(ms + RMS_EPS) * g_ref[...]).astype(o_ref.dtype)


def _rmsnorm(x, g):
    s, d = x.shape
    tr = _pick(s, (512, 256, 128))
    return pl.pallas_call(
        _rmsnorm_kernel,
        grid=(s // tr,),
        in_specs=[pl.BlockSpec((tr, d), lambda i: (i, 0)),
                  pl.BlockSpec((1, d), lambda i: (0, 0))],
        out_specs=pl.BlockSpec((tr, d), lambda i: (i, 0)),
        out_shape=jax.ShapeDtypeStruct((s, d), BF16),
        compiler_params=_params("parallel"),
        name="rmsnorm",
    )(x, g.reshape(1, d))


def _gelu(x):
    return 0.5 * x * (1.0 + lax.erf(x * np.float32(np.sqrt(0.5))))


def _sigmoid(x):
    return 0.5 * (jnp.tanh(0.5 * x) + 1.0)


BF16_SUBLANES = 16


def _cast_side(refs, n_side):
    if n_side:
        for src, dst in zip(refs[-2 * n_side - 1:-n_side - 1], refs[-n_side:]):
            dst[...] = src[...].astype(dst.dtype)


def _side_specs(side, gi, gj):
    steps = gi * gj
    in_specs, out_specs, out_shapes = [], [], []
    for w in side:
        rows = w.shape[0] // steps
        assert rows * steps == w.shape[0] and rows % BF16_SUBLANES == 0
        spec = pl.BlockSpec((rows, w.shape[1]), lambda i, j: (i * gj + j, 0))
        in_specs.append(spec)
        out_specs.append(spec)
        out_shapes.append(jax.ShapeDtypeStruct(w.shape, BF16))
    return in_specs, out_specs, out_shapes


def _split_side(side, steps):
    rides = [w.shape[0] % (steps * BF16_SUBLANES) == 0 for w in side]
    return [w for w, r in zip(side, rides) if r], rides


def _merge_side(side, rides, cast):
    cast = list(cast)
    return [cast.pop(0) if r else w.astype(BF16) for w, r in zip(side, rides)]


def _proj_kernel(x_ref, w_ref, *rest, epilogue, n_side):
    o_ref = rest[-n_side - 1]
    acc = jnp.dot(x_ref[...], w_ref[...], preferred_element_type=F32)
    if epilogue == "gelu":
        out = _gelu(acc)
    elif epilogue == "sigmoid":
        out = _sigmoid(acc)
    elif epilogue == "headnorm":
        gain_ref = rest[0]
        parts = []
        for h in range(acc.shape[1] // NA_HEAD_DIM):
            blk = acc[:, h * NA_HEAD_DIM:(h + 1) * NA_HEAD_DIM]
            ms = jnp.mean(blk * blk, axis=-1, keepdims=True)
            parts.append(blk * lax.rsqrt(ms + RMS_EPS))
        out = jnp.concatenate(parts, axis=1) * gain_ref[...]
    else:
        out = acc
    o_ref[...] = out.astype(o_ref.dtype)
    _cast_side(rest, n_side)


def _project(xn, w, col_start, n_cols, epilogue, gain=None, side=()):
    s, d = xn.shape
    tm = _pick(s, (1024, 512, 256, 128))
    tn = _pick(int(np.gcd(n_cols, col_start)), (1024, 512, 256, 128))
    off = col_start // tn
    gi, gj = s // tm, n_cols // tn
    in_specs = [pl.BlockSpec((tm, d), lambda i, j: (i, 0)),
                pl.BlockSpec((d, tn), lambda i, j: (0, j + off))]
    args = [xn, w]
    if gain is not None:
        in_specs.append(pl.BlockSpec((1, tn), lambda i, j: (0, j)))
        args.append(gain)
    riding, rides = _split_side(side, gi * gj)
    side_in, side_out, side_shapes = _side_specs(riding, gi, gj)
    outs = pl.pallas_call(
        functools.partial(_proj_kernel, epilogue=epilogue, n_side=len(riding)),
        grid=(gi, gj),
        in_specs=in_specs + side_in,
        out_specs=[pl.BlockSpec((tm, tn), lambda i, j: (i, j))] + side_out,
        out_shape=[jax.ShapeDtypeStruct((s, n_cols), BF16)] + side_shapes,
        compiler_params=_params("parallel", "parallel"),
        name="proj_" + epilogue,
    )(*args, *riding)
    return outs[0], _merge_side(side, rides, outs[1:])


def _gmlp_kernel(u_ref, v_ref, lng_ref, lnb_ref, ws_ref, bs_ref, o_ref):
    v = v_ref[...].astype(F32)
    mu = jnp.mean(v, axis=-1, keepdims=True)
    xc = v - mu
    var = jnp.mean(xc * xc, axis=-1, keepdims=True)
    vn = (xc * lax.rsqrt(var + LN_EPS) * lng_ref[...] + lnb_ref[...]).astype(BF16)
    t, width = vn.shape
    for c in range(t // CHUNK):
        rows = slice(c * CHUNK, (c + 1) * CHUNK)
        for g in range(width // GM_GROUP_DIM):
            cols = slice(g * GM_GROUP_DIM, (g + 1) * GM_GROUP_DIM)
            mixed = jnp.dot(ws_ref[g], vn[rows, cols], preferred_element_type=F32) + bs_ref[:, cols]
            o_ref[rows, cols] = (u_ref[rows, cols].astype(F32) * mixed).astype(o_ref.dtype)


def _gmlp(uv, ln_g, ln_b, w_s, b_s):
    s = uv.shape[0]
    width = uv.shape[1] // 2
    groups = width // GM_GROUP_DIM
    t = _pick(s, (512, 256, 128))
    bs_full = jnp.repeat(b_s.T, GM_GROUP_DIM, axis=1)
    return pl.pallas_call(
        _gmlp_kernel,
        grid=(s // t,),
        in_specs=[pl.BlockSpec((t, width), lambda i: (i, 0)),
                  pl.BlockSpec((t, width), lambda i: (i, 1)),
                  pl.BlockSpec((1, width), lambda i: (0, 0)),
                  pl.BlockSpec((1, width), lambda i: (0, 0)),
                  pl.BlockSpec((groups, CHUNK, CHUNK), lambda i: (0, 0, 0)),
                  pl.BlockSpec((CHUNK, width), lambda i: (0, 0))],
        out_specs=pl.BlockSpec((t, width), lambda i: (i, 0)),
        out_shape=jax.ShapeDtypeStruct((s, width), BF16),
        compiler_params=_params("parallel"),
        name="gmlp",
    )(uv, uv, ln_g.reshape(1, width), ln_b.reshape(1, width), w_s.astype(BF16), bs_full)


NA_MASKED_OFFSET = 2 * NA_WIN_H_MAX - 1
NA_PAIR = 2
NA_GROUP = 4


def _na_strip_plan(rows):
    kh = min(NA_WIN_H_MAX, rows)
    assert kh == NA_WIN_H_MAX and rows % NA_Q_ROWS == 0 and rows >= 2 * NA_K_ROWS
    assert NA_PAIR * GRID_W == 128 and NA_K_ROWS % NA_PAIR == 0
    starts = ((0, 0), (NA_Q_ROWS, 0), (rows - NA_Q_ROWS, rows - NA_K_ROWS))
    pairs, plan = [], []
    for r0, kb0 in starts:
        strips = []
        for rl in range(NA_Q_ROWS):
            r = r0 + rl
            rs = int(np.clip(r - kh // 2, 0, rows - kh))
            offs = [kb0 + krl - r + (NA_WIN_H_MAX - 1) if rs <= kb0 + krl < rs + kh else NA_MASKED_OFFSET
                    for krl in range(NA_K_ROWS)]
            row_pairs = [tuple(offs[j:j + NA_PAIR]) for j in range(0, NA_K_ROWS, NA_PAIR)]
            live = [j for j, pr in enumerate(row_pairs) if any(o != NA_MASKED_OFFSET for o in pr)]
            lo, hi = live[0], live[-1] + 1
            ids = []
            for pr in row_pairs[lo:hi]:
                if pr not in pairs:
                    pairs.append(pr)
                ids.append(pairs.index(pr))
            strips.append((lo, hi, tuple(ids)))
        plan.append(tuple(strips))
    return tuple(plan), np.asarray(pairs, np.int32)


def _na_offsets(qb, n_blocks):
    tq = NA_Q_ROWS * GRID_W
    kb = jnp.clip(qb - 1, 0, n_blocks - NA_K_ROWS // NA_Q_ROWS)
    return pl.multiple_of(qb * tq, tq), pl.multiple_of(kb * tq, tq)


def _na_scores(q_ref, k_ref, s_ref, qb, n_blocks):
    q0, k0 = _na_offsets(qb, n_blocks)
    q = q_ref[pl.ds(q0, NA_Q_ROWS * GRID_W), :]
    k = k_ref[pl.ds(k0, NA_K_ROWS * GRID_W), :]
    s_ref[...] = lax.dot_general(q, k, (((1,), (1,)), ((), ())), preferred_element_type=F32)


def _na_softmax_pv(s_ref, v_ref, pair_ref, o_ref, qb, n_blocks, strips):
    tq = NA_Q_ROWS * GRID_W
    tk = NA_K_ROWS * GRID_W
    lanes = NA_PAIR * GRID_W
    q0, k0 = _na_offsets(qb, n_blocks)
    v = v_ref[pl.ds(k0, tk), :]
    p_rows, l_rows = [], []
    for rl, (lo, hi, ids) in enumerate(strips):
        bias = jnp.concatenate([pair_ref[i] for i in ids], axis=1)
        s_rl = s_ref[rl * GRID_W:(rl + 1) * GRID_W, lo * lanes:hi * lanes] + bias
        m = jnp.max(s_rl, axis=-1, keepdims=True)
        p = jnp.exp(s_rl - m)
        l_rows.append(jnp.sum(p, axis=-1, keepdims=True))
        pieces = [p.astype(BF16)]
        if lo > 0:
            pieces.insert(0, jnp.zeros((GRID_W, lo * lanes), BF16))
        if hi * lanes < tk:
            pieces.append(jnp.zeros((GRID_W, tk - hi * lanes), BF16))
        p_rows.append(jnp.concatenate(pieces, axis=1))
    p = jnp.concatenate(p_rows, axis=0)
    o = jnp.dot(p, v, preferred_element_type=F32) / jnp.concatenate(l_rows, axis=0)
    o_ref[pl.ds(q0, tq), :] = o.astype(o_ref.dtype)


def _na_kernel(q_ref, k_ref, v_ref, pair_ref, o_ref, s_even, s_odd, *, n_blocks, group, plan):
    first, interior, last = plan
    n_groups = n_blocks // group
    bufs = (s_even, s_odd)

    def scores(g, buf):
        for u in range(group):
            _na_scores(q_ref, k_ref, buf.at[u], g * group + u, n_blocks)

    def finish(g, buf, strips):
        for u in range(group):
            _na_softmax_pv(buf.at[u], v_ref, pair_ref, o_ref, g * group + u, n_blocks, strips[u])

    def strips_of(g):
        strips = [interior] * group
        if g == 0:
            strips[0] = first
        if g == n_groups - 1:
            strips[-1] = last
        return strips

    def step(g, parity, strips):
        finish(g, bufs[parity], strips)
        scores(g + 1, bufs[1 - parity])

    scores(0, s_even)
    step(0, 0, strips_of(0))
    n_mid = n_groups - 2

    def body(t, carry):
        g = 1 + 2 * t
        step(g, 1, [interior] * group)
        step(g + 1, 0, [interior] * group)
        return carry

    lax.fori_loop(0, n_mid // 2, body, 0)
    if n_mid % 2:
        step(n_groups - 2, (n_groups - 2) % 2, strips_of(n_groups - 2))
    finish(n_groups - 1, bufs[(n_groups - 1) % 2], strips_of(n_groups - 1))


def _neighbourhood_attention(qk, v, rpb):
    s = v.shape[0]
    width = v.shape[1]
    heads = width // NA_HEAD_DIM
    rows = s // GRID_W
    n_blocks = rows // NA_Q_ROWS
    plan, pairs = _na_strip_plan(rows)
    c = np.arange(GRID_W)
    cs = np.clip(c - NA_WIN_W // 2, 0, GRID_W - NA_WIN_W)
    col_ok = (c[None, :] >= cs[:, None]) & (c[None, :] < cs[:, None] + NA_WIN_W)
    rel = c[None, :] - c[:, None] + (NA_WIN_W - 1)
    onehot = ((rel[None] == np.arange(2 * NA_WIN_W - 1)[:, None, None]) & col_ok[None]).astype(np.float32)
    tiles = jnp.einsum("hab,bck->hack", rpb.astype(F32), onehot, precision=lax.Precision.HIGHEST)
    tiles = jnp.where(col_ok[None, None], tiles, np.float32(MASK_VALUE))
    masked = jnp.full((heads, GRID_W, GRID_W), MASK_VALUE, F32)
    tile_of = lambda a: masked if a == NA_MASKED_OFFSET else tiles[:, a]
    pair_tiles = jnp.stack([jnp.concatenate([tile_of(int(a)) for a in pr], axis=-1) for pr in pairs],
                           axis=1)
    n_pairs = pairs.shape[0]
    group = _pick(n_blocks // 2, (NA_GROUP, 2, 1))
    s_buf = pltpu.VMEM((group, NA_Q_ROWS * GRID_W, NA_K_ROWS * GRID_W), F32)
    return pl.pallas_call(
        functools.partial(_na_kernel, n_blocks=n_blocks, group=group, plan=plan),
        grid=(heads,),
        scratch_shapes=[s_buf, s_buf],
        in_specs=[pl.BlockSpec((s, NA_HEAD_DIM), lambda h: (0, h)),
                  pl.BlockSpec((s, NA_HEAD_DIM), lambda h: (0, heads + h)),
                  pl.BlockSpec((s, NA_HEAD_DIM), lambda h: (0, h)),
                  pl.BlockSpec((None, n_pairs, GRID_W, NA_PAIR * GRID_W), lambda h: (h, 0, 0, 0))],
        out_specs=pl.BlockSpec((s, NA_HEAD_DIM), lambda h: (0, h)),
        out_shape=jax.ShapeDtypeStruct((s, width), BF16),
        compiler_params=_params("parallel"),
        name="natten",
    )(qk, qk, v, pair_tiles)


def _merge_kernel(ya_ref, yb_ref, wa_ref, wb_ref, ga_ref, gb_ref, o_ref):
    a = jnp.dot(ya_ref[...], wa_ref[...], preferred_element_type=F32)
    b = jnp.dot(yb_ref[...], wb_ref[...], preferred_element_type=F32)
    o_ref[...] = (ga_ref[...].astype(F32) * a + gb_ref[...].astype(F32) * b).astype(o_ref.dtype)


def _merge(y_a, y_b, w_oa, w_ob, sg):
    s, k = y_a.shape
    n = w_oa.shape[1]
    tm = _pick(s, (1024, 512, 256, 128))
    tn = _pick(n, (1024, 512, 256, 128))
    gb_off = n // tn
    return pl.pallas_call(
        _merge_kernel,
        grid=(s // tm, n // tn),
        in_specs=[pl.BlockSpec((tm, k), lambda i, j: (i, 0)),
                  pl.BlockSpec((tm, k), lambda i, j: (i, 0)),
                  pl.BlockSpec((k, tn), lambda i, j: (0, j)),
                  pl.BlockSpec((k, tn), lambda i, j: (0, j)),
                  pl.BlockSpec((tm, tn), lambda i, j: (i, j)),
                  pl.BlockSpec((tm, tn), lambda i, j: (i, j + gb_off))],
        out_specs=pl.BlockSpec((tm, tn), lambda i, j: (i, j)),
        out_shape=jax.ShapeDtypeStruct((s, n), BF16),
        compiler_params=_params("parallel", "parallel"),
        name="merge",
    )(y_a, y_b, w_oa, w_ob, sg, sg)


def _matmul_residual_kernel(x_ref, w_ref, r_ref, o_ref):
    o_ref[...] = r_ref[...] + jnp.dot(x_ref[...], w_ref[...], preferred_element_type=F32)


def _matmul_residual(lhs, w, residual, tm_prefs, tn_prefs, name):
    s, k = lhs.shape
    n = w.shape[1]
    tm = _pick(s, tm_prefs)
    tn = _pick(n, tn_prefs)
    return pl.pallas_call(
        _matmul_residual_kernel,
        grid=(s // tm, n // tn),
        in_specs=[pl.BlockSpec((tm, k), lambda i, j: (i, 0)),
                  pl.BlockSpec((k, tn), lambda i, j: (0, j)),
                  pl.BlockSpec((tm, tn), lambda i, j: (i, j))],
        out_specs=pl.BlockSpec((tm, tn), lambda i, j: (i, j)),
        out_shape=jax.ShapeDtypeStruct((s, n), F32),
        compiler_params=_params("parallel", "parallel"),
        name=name,
    )(lhs, w, residual)


def _swiglu_kernel(x_ref, wg_ref, wu_ref, *rest, n_side):
    o_ref = rest[-n_side - 1]
    x = x_ref[...]
    g = jnp.dot(x, wg_ref[...], preferred_element_type=F32)
    u = jnp.dot(x, wu_ref[...], preferred_element_type=F32)
    o_ref[...] = (g * _sigmoid(g) * u).astype(o_ref.dtype)
    _cast_side(rest, n_side)


def _swiglu_up(hn, w_gate, w_up, side=()):
    s, d = hn.shape
    f = w_gate.shape[1]
    tm = _pick(s, (2048, 1024, 512, 256, 128))
    tn = _pick(f, (256, 128))
    gi, gj = s // tm, f // tn
    riding, rides = _split_side(side, gi * gj)
    side_in, side_out, side_shapes = _side_specs(riding, gi, gj)
    outs = pl.pallas_call(
        functools.partial(_swiglu_kernel, n_side=len(riding)),
        grid=(gi, gj),
        in_specs=[pl.BlockSpec((tm, d), lambda i, j: (i, 0)),
                  pl.BlockSpec((d, tn), lambda i, j: (0, j)),
                  pl.BlockSpec((d, tn), lambda i, j: (0, j))] + side_in,
        out_specs=[pl.BlockSpec((tm, tn), lambda i, j: (i, j))] + side_out,
        out_shape=[jax.ShapeDtypeStruct((s, f), BF16)] + side_shapes,
        compiler_params=_params("parallel", "parallel"),
        name="swiglu_up",
    )(hn, w_gate, w_up, *riding)
    return outs[0], _merge_side(side, rides, outs[1:])


def kernel(x, norm1_g, w_in, gm_ln_g, gm_ln_b, gm_w_s, gm_b_s, q_gain, k_gain, na_rpb,
           w_o_gm, w_o_na, w_out, norm2_g, w_ff_gate, w_ff_up, w_ff_down):
    batch, seq, d_model = x.shape
    depth = w_in.shape[0]
    gm_width = gm_ln_g.shape[1]
    na_width = w_o_na.shape[1]
    heads = na_width // NA_HEAD_DIM
    outs = []
    for bi in range(batch):
        h = x.reshape(seq, d_model) if batch == 1 else x[bi]
        for layer in range(depth):
            xn = _rmsnorm(h, norm1_g[layer])
            w_uv = w_in[layer][:, :2 * gm_width].astype(BF16)
            uv, (w_in_l,) = _project(xn, w_uv, 0, 2 * gm_width, "gelu", side=(w_in[layer],))
            gains = jnp.concatenate([jnp.tile(q_gain[layer] * np.float32(NA_HEAD_DIM ** -0.5), heads),
                                     jnp.tile(k_gain[layer], heads)]).reshape(1, 2 * na_width)
            qk, (w_oa, w_ob, w_o) = _project(xn, w_in_l, 2 * gm_width, 2 * na_width, "headnorm", gains,
                                             side=(w_o_gm[layer], w_o_na[layer], w_out[layer]))
            v, _ = _project(xn, w_in_l, 2 * gm_width + 2 * na_width, na_width, "none")
            sg, (w_gate, w_up) = _project(xn, w_in_l, 2 * gm_width + 3 * na_width, 2 * d_model, "sigmoid",
                                          side=(w_ff_gate[layer], w_ff_up[layer]))
            y_a = _gmlp(uv, gm_ln_g[layer], gm_ln_b[layer], gm_w_s[layer], gm_b_s[layer])
            y_b = _neighbourhood_attention(qk, v, na_rpb[layer])
            merged = _merge(y_a, y_b, w_oa, w_ob, sg)
            h = _matmul_residual(merged, w_o, h,
                                 (1024, 512, 256, 128), (1024, 512, 256, 128), "out_proj")
            hn = _rmsnorm(h, norm2_g[layer])
            act, (w_down,) = _swiglu_up(hn, w_gate, w_up, side=(w_ff_down[layer],))
            h = _matmul_residual(act, w_down, h,
                                 (512, 256, 128), (512, 256, 128), "ffn_down")
        outs.append(h)
    if batch == 1:
        return outs[0].reshape(1, seq, d_model)
    return jnp.stack(outs)
```

```python
import functools

import numpy as np
import jax
import jax.numpy as jnp
from jax import lax
from jax.experimental import pallas as pl
from jax.experimental.pallas import tpu as pltpu

GRID_W = 64
CHUNK = 128
GM_GROUP_DIM = 128
NA_HEAD_DIM = 128
NA_WIN_H_MAX = 8
NA_WIN_W = 16
RMS_EPS = 1e-6
LN_EPS = 1e-5

V7X_VMEM_BYTES = 64 * 1024 * 1024
VMEM_LIMIT_BYTES = V7X_VMEM_BYTES - 6 * 1024 * 1024

NA_Q_ROWS = 4
NA_K_ROWS = NA_Q_ROWS + NA_WIN_H_MAX
MASK_VALUE = -1e30

F32 = jnp.float32
BF16 = jnp.bfloat16


def _params(*sem):
    return pltpu.CompilerParams(dimension_semantics=sem, vmem_limit_bytes=VMEM_LIMIT_BYTES)


def _pick(n, prefs):
    for p in prefs:
        if n % p == 0:
            return p
    return n


def _rmsnorm_kernel(x_ref, g_ref, o_ref):
    x = x_ref[...]
    ms = jnp.mean(x * x, axis=-1, keepdims=True)
    o_ref[...] = (x * lax.rsqrt(ms + RMS_EPS) * g_ref[...]).astype(o_ref.dtype)


def _rmsnorm(x, g):
    s, d = x.shape
    tr = _pick(s, (512, 256, 128))
    return pl.pallas_call(
        _rmsnorm_kernel,
        grid=(s // tr,),
        in_specs=[pl.BlockSpec((tr, d), lambda i: (i, 0)),
                  pl.BlockSpec((1, d), lambda i: (0, 0))],
        out_specs=pl.BlockSpec((tr, d), lambda i: (i, 0)),
        out_shape=jax.ShapeDtypeStruct((s, d), BF16),
        compiler_params=_params("parallel"),
        name="rmsnorm",
    )(x, g.reshape(1, d))


def _gelu(x):
    return 0.5 * x * (1.0 + lax.erf(x * np.float32(np.sqrt(0.5))))


def _sigmoid(x):
    return 0.5 * (jnp.tanh(0.5 * x) + 1.0)


BF16_SUBLANES = 16


class _CastRider:
    def __init__(self, weights, gi, gj):
        steps = gi * gj
        self.weights = list(weights)
        self.rides = [w.shape[0] % (steps * BF16_SUBLANES) == 0 for w in self.weights]
        self.args = [w for w, r in zip(self.weights, self.rides) if r]
        self.in_specs = [pl.BlockSpec((w.shape[0] // steps, w.shape[1]), lambda i, j: (i * gj + j, 0))
                         for w in self.args]
        self.out_specs = list(self.in_specs)
        self.out_shapes = [jax.ShapeDtypeStruct(w.shape, BF16) for w in self.args]

    def body(self, in_refs, out_refs):
        for src, dst in zip(in_refs, out_refs):
            dst[...] = src[...].astype(dst.dtype)

    def results(self, outs):
        outs = list(outs)
        return [outs.pop(0) if r else w.astype(BF16) for w, r in zip(self.weights, self.rides)]


def _rider_specs(riders):
    cat = lambda name: [x for r in riders for x in getattr(r, name)]
    return cat("args"), cat("in_specs"), cat("out_specs"), cat("out_shapes")


def _split_refs(rest, riders):
    n_in = sum(len(r.in_specs) for r in riders)
    n_out = sum(len(r.out_specs) for r in riders)
    own = rest[:len(rest) - n_in - 1 - n_out]
    return own, rest[len(own) + n_in], rest[len(own):len(own) + n_in], rest[len(own) + n_in + 1:]


def _run_riders(riders, in_refs, out_refs):
    for r in riders:
        n_in, n_out = len(r.in_specs), len(r.out_specs)
        r.body(in_refs[:n_in], out_refs[:n_out])
        in_refs, out_refs = in_refs[n_in:], out_refs[n_out:]


def _rider_results(riders, outs):
    outs, res = list(outs), []
    for r in riders:
        n_out = len(r.out_specs)
        res.append(r.results(outs[:n_out]))
        outs = outs[n_out:]
    return res


def _proj_kernel(x_ref, w_ref, *rest, epilogue, riders):
    rest, o_ref, rider_in, rider_out = _split_refs(rest, riders)
    acc = jnp.dot(x_ref[...], w_ref[...], preferred_element_type=F32)
    if epilogue == "gelu":
        out = _gelu(acc)
    elif epilogue == "sigmoid":
        out = _sigmoid(acc)
    elif epilogue == "headnorm":
        gain_ref = rest[0]
        parts = []
        for h in range(acc.shape[1] // NA_HEAD_DIM):
            blk = acc[:, h * NA_HEAD_DIM:(h + 1) * NA_HEAD_DIM]
            ms = jnp.mean(blk * blk, axis=-1, keepdims=True)
            parts.append(blk * lax.rsqrt(ms + RMS_EPS))
        out = jnp.concatenate(parts, axis=1) * gain_ref[...]
    else:
        out = acc
    o_ref[...] = out.astype(o_ref.dtype)
    _run_riders(riders, rider_in, rider_out)


def _project(xn, w, col_start, n_cols, epilogue, gain=None, riders=()):
    s, d = xn.shape
    tm = _pick(s, (1024, 512, 256, 128))
    tn = _pick(int(np.gcd(n_cols, col_start)), (1024, 512, 256, 128))
    off = col_start // tn
    gi, gj = s // tm, n_cols // tn
    in_specs = [pl.BlockSpec((tm, d), lambda i, j: (i, 0)),
                pl.BlockSpec((d, tn), lambda i, j: (0, j + off))]
    args = [xn, w]
    if gain is not None:
        in_specs.append(pl.BlockSpec((1, tn), lambda i, j: (0, j)))
        args.append(gain)
    riders = tuple(make(gi, gj) for make in riders)
    r_args, r_in, r_out, r_shapes = _rider_specs(riders)
    outs = pl.pallas_call(
        functools.partial(_proj_kernel, epilogue=epilogue, riders=riders),
        grid=(gi, gj),
        in_specs=in_specs + r_in,
        out_specs=[pl.BlockSpec((tm, tn), lambda i, j: (i, j))] + r_out,
        out_shape=[jax.ShapeDtypeStruct((s, n_cols), BF16)] + r_shapes,
        compiler_params=_params("parallel", "parallel"),
        name="proj_" + epilogue,
    )(*args, *r_args)
    return (outs[0], *_rider_results(riders, outs[1:]))


def _gmlp_kernel(u_ref, v_ref, lng_ref, lnb_ref, ws_ref, bs_ref, o_ref):
    v = v_ref[...].astype(F32)
    mu = jnp.mean(v, axis=-1, keepdims=True)
    xc = v - mu
    var = jnp.mean(xc * xc, axis=-1, keepdims=True)
    vn = (xc * lax.rsqrt(var + LN_EPS) * lng_ref[...] + lnb_ref[...]).astype(BF16)
    t, width = vn.shape
    fused = ws_ref.shape[2] // CHUNK
    span = fused * GM_GROUP_DIM
    zero = jnp.zeros((CHUNK, GM_GROUP_DIM), BF16)
    for c in range(t // CHUNK):
        rows = slice(c * CHUNK, (c + 1) * CHUNK)
        for p in range(width // span):
            cols = slice(p * span, (p + 1) * span)
            blocks = [vn[rows, p * span + q * GM_GROUP_DIM:p * span + (q + 1) * GM_GROUP_DIM]
                      for q in range(fused)]
            rhs = jnp.concatenate(
                [jnp.concatenate([blocks[q] if q2 == q else zero for q2 in range(fused)], axis=1)
                 for q in range(fused)], axis=0)
            mixed = jnp.dot(ws_ref[p], rhs, preferred_element_type=F32) + bs_ref[:, cols]
            o_ref[rows, cols] = (u_ref[rows, cols].astype(F32) * mixed).astype(o_ref.dtype)


GM_FUSED_GROUPS = 2


def _gmlp_operands(uv, ln_g, ln_b, w_s, b_s):
    width = uv.shape[1] // 2
    groups = w_s.shape[0]
    fused = GM_FUSED_GROUPS if groups % GM_FUSED_GROUPS == 0 else 1
    ws = jnp.concatenate([w_s[q::fused] for q in range(fused)], axis=2).astype(BF16)
    bs_full = jnp.repeat(b_s.T, GM_GROUP_DIM, axis=1)
    return [uv, uv, ln_g.reshape(1, width), ln_b.reshape(1, width), ws, bs_full]


def _gmlp_specs(t, operands, step):
    width = operands[0].shape[1] // 2
    const = lambda x: pl.BlockSpec(x.shape, lambda *idx: (0,) * x.ndim)
    in_specs = [pl.BlockSpec((t, width), lambda *idx: (step(*idx), 0)),
                pl.BlockSpec((t, width), lambda *idx: (step(*idx), 1))] + [const(x) for x in operands[2:]]
    return in_specs, pl.BlockSpec((t, width), lambda *idx: (step(*idx), 0))


def _gmlp(uv, ln_g, ln_b, w_s, b_s):
    s = uv.shape[0]
    width = uv.shape[1] // 2
    t = _pick(s, (512, 256, 128))
    operands = _gmlp_operands(uv, ln_g, ln_b, w_s, b_s)
    in_specs, out_spec = _gmlp_specs(t, operands, lambda i: i)
    return pl.pallas_call(
        _gmlp_kernel,
        grid=(s // t,),
        in_specs=in_specs,
        out_specs=out_spec,
        out_shape=jax.ShapeDtypeStruct((s, width), BF16),
        compiler_params=_params("parallel"),
        name="gmlp",
    )(*operands)


class _GmlpRider:
    def __init__(self, operands, gi, gj):
        uv = operands[0]
        s, width = uv.shape[0], uv.shape[1] // 2
        self.operands = operands
        t = s // (gi * gj)
        self.rides = t * gi * gj == s and t % CHUNK == 0
        self.args, self.in_specs, self.out_specs, self.out_shapes = [], [], [], []
        if self.rides:
            self.args = _gmlp_operands(*operands)
            self.in_specs, out_spec = _gmlp_specs(t, self.args, lambda i, j: i * gj + j)
            self.out_specs = [out_spec]
            self.out_shapes = [jax.ShapeDtypeStruct((s, width), BF16)]

    def body(self, in_refs, out_refs):
        if self.rides:
            _gmlp_kernel(*in_refs, *out_refs)

    def results(self, outs):
        return outs[0] if self.rides else _gmlp(*self.operands)


NA_MASKED_OFFSET = 2 * NA_WIN_H_MAX - 1
NA_PAIR = 2
NA_GROUP = 4


def _na_strip_plan(rows):
    kh = min(NA_WIN_H_MAX, rows)
    assert kh == NA_WIN_H_MAX and rows % NA_Q_ROWS == 0 and rows >= 2 * NA_K_ROWS
    assert NA_PAIR * GRID_W == 128 and NA_K_ROWS % NA_PAIR == 0
    starts = ((0, 0), (NA_Q_ROWS, 0), (rows - NA_Q_ROWS, rows - NA_K_ROWS))
    pairs, plan = [], []
    for r0, kb0 in starts:
        strips = []
        for rl in range(NA_Q_ROWS):
            r = r0 + rl
            rs = int(np.clip(r - kh // 2, 0, rows - kh))
            offs = [kb0 + krl - r + (NA_WIN_H_MAX - 1) if rs <= kb0 + krl < rs + kh else NA_MASKED_OFFSET
                    for krl in range(NA_K_ROWS)]
            row_pairs = [tuple(offs[j:j + NA_PAIR]) for j in range(0, NA_K_ROWS, NA_PAIR)]
            live = [j for j, pr in enumerate(row_pairs) if any(o != NA_MASKED_OFFSET for o in pr)]
            lo, hi = live[0], live[-1] + 1
            ids = []
            for pr in row_pairs[lo:hi]:
                if pr not in pairs:
                    pairs.append(pr)
                ids.append(pairs.index(pr))
            strips.append((lo, hi, tuple(ids)))
        plan.append(tuple(strips))
    return tuple(plan), np.asarray(pairs, np.int32)


def _na_offsets(qb, n_blocks):
    tq = NA_Q_ROWS * GRID_W
    kb = jnp.clip(qb - 1, 0, n_blocks - NA_K_ROWS // NA_Q_ROWS)
    return pl.multiple_of(qb * tq, tq), pl.multiple_of(kb * tq, tq)


def _na_scores(q_ref, k_ref, s_ref, qb, n_blocks):
    q0, k0 = _na_offsets(qb, n_blocks)
    q = q_ref[pl.ds(q0, NA_Q_ROWS * GRID_W), :]
    k = k_ref[pl.ds(k0, NA_K_ROWS * GRID_W), :]
    s_ref[...] = lax.dot_general(q, k, (((1,), (1,)), ((), ())), preferred_element_type=F32)


def _na_softmax_pv(s_ref, v_ref, pair_ref, o_ref, qb, n_blocks, strips):
    tq = NA_Q_ROWS * GRID_W
    tk = NA_K_ROWS * GRID_W
    lanes = NA_PAIR * GRID_W
    q0, k0 = _na_offsets(qb, n_blocks)
    v = v_ref[pl.ds(k0, tk), :]
    p_rows, l_rows = [], []
    for rl, (lo, hi, ids) in enumerate(strips):
        bias = jnp.concatenate([pair_ref[i] for i in ids], axis=1)
        s_rl = s_ref[rl * GRID_W:(rl + 1) * GRID_W, lo * lanes:hi * lanes] + bias
        m = jnp.max(s_rl, axis=-1, keepdims=True)
        p = jnp.exp(s_rl - m)
        l_rows.append(jnp.sum(p, axis=-1, keepdims=True))
        pieces = [p.astype(BF16)]
        if lo > 0:
            pieces.insert(0, jnp.zeros((GRID_W, lo * lanes), BF16))
        if hi * lanes < tk:
            pieces.append(jnp.zeros((GRID_W, tk - hi * lanes), BF16))
        p_rows.append(jnp.concatenate(pieces, axis=1))
    p = jnp.concatenate(p_rows, axis=0)
    o = jnp.dot(p, v, preferred_element_type=F32) / jnp.concatenate(l_rows, axis=0)
    o_ref[pl.ds(q0, tq), :] = o.astype(o_ref.dtype)


def _na_kernel(q_ref, k_ref, v_ref, pair_ref, o_ref, s_even, s_odd, *, n_blocks, group, plan):
    first, interior, last = plan
    n_groups = n_blocks // group
    bufs = (s_even, s_odd)

    def scores(g, buf):
        for u in range(group):
            _na_scores(q_ref, k_ref, buf.at[u], g * group + u, n_blocks)

    def finish(g, buf, strips):
        for u in range(group):
            _na_softmax_pv(buf.at[u], v_ref, pair_ref, o_ref, g * group + u, n_blocks, strips[u])

    def strips_of(g):
        strips = [interior] * group
        if g == 0:
            strips[0] = first
        if g == n_groups - 1:
            strips[-1] = last
        return strips

    def step(g, parity, strips):
        finish(g, bufs[parity], strips)
        scores(g + 1, bufs[1 - parity])

    scores(0, s_even)
    step(0, 0, strips_of(0))
    n_mid = n_groups - 2

    def body(t, carry):
        g = 1 + 2 * t
        step(g, 1, [interior] * group)
        step(g + 1, 0, [interior] * group)
        return carry

    lax.fori_loop(0, n_mid // 2, body, 0)
    if n_mid % 2:
        step(n_groups - 2, (n_groups - 2) % 2, strips_of(n_groups - 2))
    finish(n_groups - 1, bufs[(n_groups - 1) % 2], strips_of(n_groups - 1))


def _neighbourhood_attention(qk, v, rpb):
    s = v.shape[0]
    width = v.shape[1]
    heads = width // NA_HEAD_DIM
    rows = s // GRID_W
    n_blocks = rows // NA_Q_ROWS
    plan, pairs = _na_strip_plan(rows)
    c = np.arange(GRID_W)
    cs = np.clip(c - NA_WIN_W // 2, 0, GRID_W - NA_WIN_W)
    col_ok = (c[None, :] >= cs[:, None]) & (c[None, :] < cs[:, None] + NA_WIN_W)
    rel = c[None, :] - c[:, None] + (NA_WIN_W - 1)
    onehot = ((rel[None] == np.arange(2 * NA_WIN_W - 1)[:, None, None]) & col_ok[None]).astype(np.float32)
    tiles = jnp.einsum("hab,bck->hack", rpb.astype(F32), onehot, precision=lax.Precision.HIGHEST)
    tiles = jnp.where(col_ok[None, None], tiles, np.float32(MASK_VALUE))
    masked = jnp.full((heads, GRID_W, GRID_W), MASK_VALUE, F32)
    tile_of = lambda a: masked if a == NA_MASKED_OFFSET else tiles[:, a]
    pair_tiles = jnp.stack([jnp.concatenate([tile_of(int(a)) for a in pr], axis=-1) for pr in pairs],
                           axis=1)
    n_pairs = pairs.shape[0]
    group = _pick(n_blocks // 2, (NA_GROUP, 2, 1))
    s_buf = pltpu.VMEM((group, NA_Q_ROWS * GRID_W, NA_K_ROWS * GRID_W), F32)
    return pl.pallas_call(
        functools.partial(_na_kernel, n_blocks=n_blocks, group=group, plan=plan),
        grid=(heads,),
        scratch_shapes=[s_buf, s_buf],
        in_specs=[pl.BlockSpec((s, NA_HEAD_DIM), lambda h: (0, h)),
                  pl.BlockSpec((s, NA_HEAD_DIM), lambda h: (0, heads + h)),
                  pl.BlockSpec((s, NA_HEAD_DIM), lambda h: (0, h)),
                  pl.BlockSpec((None, n_pairs, GRID_W, NA_PAIR * GRID_W), lambda h: (h, 0, 0, 0))],
        out_specs=pl.BlockSpec((s, NA_HEAD_DIM), lambda h: (0, h)),
        out_shape=jax.ShapeDtypeStruct((s, width), BF16),
        compiler_params=_params("parallel"),
        name="natten",
    )(qk, qk, v, pair_tiles)


def _merge_kernel(ya_ref, yb_ref, wa_ref, wb_ref, ga_ref, gb_ref, o_ref):
    a = jnp.dot(ya_ref[...], wa_ref[...], preferred_element_type=F32)
    b = jnp.dot(yb_ref[...], wb_ref[...], preferred_element_type=F32)
    o_ref[...] = (ga_ref[...].astype(F32) * a + gb_ref[...].astype(F32) * b).astype(o_ref.dtype)


def _merge(y_a, y_b, w_oa, w_ob, sg):
    s, k = y_a.shape
    n = w_oa.shape[1]
    tm = _pick(s, (1024, 512, 256, 128))
    tn = _pick(n, (1024, 512, 256, 128))
    gb_off = n // tn
    return pl.pallas_call(
        _merge_kernel,
        grid=(s // tm, n // tn),
        in_specs=[pl.BlockSpec((tm, k), lambda i, j: (i, 0)),
                  pl.BlockSpec((tm, k), lambda i, j: (i, 0)),
                  pl.BlockSpec((k, tn), lambda i, j: (0, j)),
                  pl.BlockSpec((k, tn), lambda i, j: (0, j)),
                  pl.BlockSpec((tm, tn), lambda i, j: (i, j)),
                  pl.BlockSpec((tm, tn), lambda i, j: (i, j + gb_off))],
        out_specs=pl.BlockSpec((tm, tn), lambda i, j: (i, j)),
        out_shape=jax.ShapeDtypeStruct((s, n), BF16),
        compiler_params=_params("parallel", "parallel"),
        name="merge",
    )(y_a, y_b, w_oa, w_ob, sg, sg)


def _matmul_residual_kernel(x_ref, w_ref, r_ref, o_ref):
    o_ref[...] = r_ref[...] + jnp.dot(x_ref[...], w_ref[...], preferred_element_type=F32)


def _matmul_residual(lhs, w, residual, tm_prefs, tn_prefs, name):
    s, k = lhs.shape
    n = w.shape[1]
    tm = _pick(s, tm_prefs)
    tn = _pick(n, tn_prefs)
    return pl.pallas_call(
        _matmul_residual_kernel,
        grid=(s // tm, n // tn),
        in_specs=[pl.BlockSpec((tm, k), lambda i, j: (i, 0)),
                  pl.BlockSpec((k, tn), lambda i, j: (0, j)),
                  pl.BlockSpec((tm, tn), lambda i, j: (i, j))],
        out_specs=pl.BlockSpec((tm, tn), lambda i, j: (i, j)),
        out_shape=jax.ShapeDtypeStruct((s, n), F32),
        compiler_params=_params("parallel", "parallel"),
        name=name,
    )(lhs, w, residual)


def _swiglu_kernel(x_ref, wg_ref, wu_ref, *rest, riders):
    _, o_ref, rider_in, rider_out = _split_refs(rest, riders)
    x = x_ref[...]
    g = jnp.dot(x, wg_ref[...], preferred_element_type=F32)
    u = jnp.dot(x, wu_ref[...], preferred_element_type=F32)
    o_ref[...] = (g * _sigmoid(g) * u).astype(o_ref.dtype)
    _run_riders(riders, rider_in, rider_out)


def _swiglu_up(hn, w_gate, w_up, riders=()):
    s, d = hn.shape
    f = w_gate.shape[1]
    tm = _pick(s, (4096, 2048, 1024, 512, 256, 128))
    tn = _pick(f, (256, 128))
    gi, gj = s // tm, f // tn
    riders = tuple(make(gi, gj) for make in riders)
    r_args, r_in, r_out, r_shapes = _rider_specs(riders)
    outs = pl.pallas_call(
        functools.partial(_swiglu_kernel, riders=riders),
        grid=(gi, gj),
        in_specs=[pl.BlockSpec((tm, d), lambda i, j: (i, 0), pipeline_mode=pl.Buffered(1)),
                  pl.BlockSpec((d, tn), lambda i, j: (0, j)),
                  pl.BlockSpec((d, tn), lambda i, j: (0, j))] + r_in,
        out_specs=[pl.BlockSpec((tm, tn), lambda i, j: (i, j))] + r_out,
        out_shape=[jax.ShapeDtypeStruct((s, f), BF16)] + r_shapes,
        compiler_params=_params("parallel", "parallel"),
        name="swiglu_up",
    )(hn, w_gate, w_up, *r_args)
    return (outs[0], *_rider_results(riders, outs[1:]))


def kernel(x, norm1_g, w_in, gm_ln_g, gm_ln_b, gm_w_s, gm_b_s, q_gain, k_gain, na_rpb,
           w_o_gm, w_o_na, w_out, norm2_g, w_ff_gate, w_ff_up, w_ff_down):
    batch, seq, d_model = x.shape
    depth = w_in.shape[0]
    gm_width = gm_ln_g.shape[1]
    na_width = w_o_na.shape[1]
    heads = na_width // NA_HEAD_DIM
    outs = []
    for bi in range(batch):
        h = x.reshape(seq, d_model) if batch == 1 else x[bi]
        for layer in range(depth):
            xn = _rmsnorm(h, norm1_g[layer])
            w_uv = w_in[layer][:, :2 * gm_width].astype(BF16)
            cast = lambda *ws: functools.partial(_CastRider, ws)
            uv, (w_in_l,) = _project(xn, w_uv, 0, 2 * gm_width, "gelu", riders=(cast(w_in[layer]),))
            gains = jnp.concatenate([jnp.tile(q_gain[layer] * np.float32(NA_HEAD_DIM ** -0.5), heads),
                                     jnp.tile(k_gain[layer], heads)]).reshape(1, 2 * na_width)
            qk, (w_oa, w_ob, w_o) = _project(xn, w_in_l, 2 * gm_width, 2 * na_width, "headnorm", gains,
                                             riders=(cast(w_o_gm[layer], w_o_na[layer], w_out[layer]),))
            v, = _project(xn, w_in_l, 2 * gm_width + 2 * na_width, na_width, "none")
            gmlp = functools.partial(
                _GmlpRider, (uv, gm_ln_g[layer], gm_ln_b[layer], gm_w_s[layer], gm_b_s[layer]))
            sg, (w_gate, w_up), y_a = _project(
                xn, w_in_l, 2 * gm_width + 3 * na_width, 2 * d_model, "sigmoid",
                riders=(cast(w_ff_gate[layer], w_ff_up[layer]), gmlp))
            y_b = _neighbourhood_attention(qk, v, na_rpb[layer])
            merged = _merge(y_a, y_b, w_oa, w_ob, sg)
            h = _matmul_residual(merged, w_o, h,
                                 (1024, 512, 256, 128), (1024, 512, 256, 128), "out_proj")
            hn = _rmsnorm(h, norm2_g[layer])
            act, (w_down,) = _swiglu_up(hn, w_gate, w_up, riders=(cast(w_ff_down[layer]),))
            h = _matmul_residual(act, w_down, h,
                                 (512, 256, 128), (512, 256, 128), "ffn_down")
        outs.append(h)
    if batch == 1:
        return outs[0].reshape(1, seq, d_model)
    return jnp.stack(outs)
```

```python
import functools

import numpy as np
import jax
import jax.numpy as jnp
from jax import lax
from jax.experimental import pallas as pl
from jax.experimental.pallas import tpu as pltpu

GRID_W = 64
CHUNK = 128
GM_GROUP_DIM = 128
NA_HEAD_DIM = 128
NA_WIN_H_MAX = 8
NA_WIN_W = 16
RMS_EPS = 1e-6
LN_EPS = 1e-5

V7X_VMEM_BYTES = 64 * 1024 * 1024
VMEM_LIMIT_BYTES = V7X_VMEM_BYTES - 6 * 1024 * 1024

NA_Q_ROWS = 4
NA_K_ROWS = NA_Q_ROWS + NA_WIN_H_MAX
MASK_VALUE = -1e30

F32 = jnp.float32
BF16 = jnp.bfloat16


def _params(*sem):
    return pltpu.CompilerParams(dimension_semantics=sem, vmem_limit_bytes=VMEM_LIMIT_BYTES)


def _pick(n, prefs):
    for p in prefs:
        if n % p == 0:
            return p
    return n


def _rmsnorm_kernel(x_ref, g_ref, o_ref):
    x = x_ref[...]
    ms = jnp.mean(x * x, axis=-1, keepdims=True)
    o_ref[...] = (x * lax.rsqrt(ms + RMS_EPS) * g_ref[...]).astype(o_ref.dtype)


def _rmsnorm(x, g):
    s, d = x.shape
    tr = _pick(s, (512, 256, 128))
    return pl.pallas_call(
        _rmsnorm_kernel,
        grid=(s // tr,),
        in_specs=[pl.BlockSpec((tr, d), lambda i: (i, 0)),
                  pl.BlockSpec((1, d), lambda i: (0, 0))],
        out_specs=pl.BlockSpec((tr, d), lambda i: (i, 0)),
        out_shape=jax.ShapeDtypeStruct((s, d), BF16),
        compiler_params=_params("parallel"),
        name="rmsnorm",
    )(x, g.reshape(1, d))


def _gelu(x):
    return 0.5 * x * (1.0 + lax.erf(x * np.float32(np.sqrt(0.5))))


def _sigmoid(x):
    return 0.5 * (jnp.tanh(0.5 * x) + 1.0)


BF16_SUBLANES = 16


class _CastRider:
    def __init__(self, weights, gi, gj):
        steps = gi * gj
        self.weights = list(weights)
        self.rides = [w.shape[0] % (steps * BF16_SUBLANES) == 0 for w in self.weights]
        self.args = [w for w, r in zip(self.weights, self.rides) if r]
        self.in_specs = [pl.BlockSpec((w.shape[0] // steps, w.shape[1]), lambda i, j: (i * gj + j, 0))
                         for w in self.args]
        self.out_specs = list(self.in_specs)
        self.out_shapes = [jax.ShapeDtypeStruct(w.shape, BF16) for w in self.args]

    def body(self, in_refs, out_refs):
        for src, dst in zip(in_refs, out_refs):
            dst[...] = src[...].astype(dst.dtype)

    def results(self, outs):
        outs = list(outs)
        return [outs.pop(0) if r else w.astype(BF16) for w, r in zip(self.weights, self.rides)]


def _rider_specs(riders):
    cat = lambda name: [x for r in riders for x in getattr(r, name)]
    return cat("args"), cat("in_specs"), cat("out_specs"), cat("out_shapes")


def _split_refs(rest, riders):
    n_in = sum(len(r.in_specs) for r in riders)
    n_out = sum(len(r.out_specs) for r in riders)
    own = rest[:len(rest) - n_in - 1 - n_out]
    return own, rest[len(own) + n_in], rest[len(own):len(own) + n_in], rest[len(own) + n_in + 1:]


def _run_riders(riders, in_refs, out_refs):
    for r in riders:
        n_in, n_out = len(r.in_specs), len(r.out_specs)
        r.body(in_refs[:n_in], out_refs[:n_out])
        in_refs, out_refs = in_refs[n_in:], out_refs[n_out:]


def _rider_results(riders, outs):
    outs, res = list(outs), []
    for r in riders:
        n_out = len(r.out_specs)
        res.append(r.results(outs[:n_out]))
        outs = outs[n_out:]
    return res


def _proj_kernel(x_ref, w_ref, *rest, epilogue, riders):
    rest, o_ref, rider_in, rider_out = _split_refs(rest, riders)
    acc = jnp.dot(x_ref[...], w_ref[...], preferred_element_type=F32)
    if epilogue == "gelu":
        out = _gelu(acc)
    elif epilogue == "sigmoid":
        out = _sigmoid(acc)
    elif epilogue == "headnorm":
        gain_ref = rest[0]
        parts = []
        for h in range(acc.shape[1] // NA_HEAD_DIM):
            blk = acc[:, h * NA_HEAD_DIM:(h + 1) * NA_HEAD_DIM]
            ms = jnp.mean(blk * blk, axis=-1, keepdims=True)
            parts.append(blk * lax.rsqrt(ms + RMS_EPS))
        out = jnp.concatenate(parts, axis=1) * gain_ref[...]
    else:
        out = acc
    o_ref[...] = out.astype(o_ref.dtype)
    _run_riders(riders, rider_in, rider_out)


def _project(xn, w, col_start, n_cols, epilogue, gain=None, riders=()):
    s, d = xn.shape
    tm = _pick(s, (1024, 512, 256, 128))
    tn = _pick(int(np.gcd(n_cols, col_start)), (1024, 512, 256, 128))
    off = col_start // tn
    gi, gj = s // tm, n_cols // tn
    in_specs = [pl.BlockSpec((tm, d), lambda i, j: (i, 0)),
                pl.BlockSpec((d, tn), lambda i, j: (0, j + off))]
    args = [xn, w]
    if gain is not None:
        in_specs.append(pl.BlockSpec((1, tn), lambda i, j: (0, j)))
        args.append(gain)
    riders = tuple(make(gi, gj) for make in riders)
    r_args, r_in, r_out, r_shapes = _rider_specs(riders)
    outs = pl.pallas_call(
        functools.partial(_proj_kernel, epilogue=epilogue, riders=riders),
        grid=(gi, gj),
        in_specs=in_specs + r_in,
        out_specs=[pl.BlockSpec((tm, tn), lambda i, j: (i, j))] + r_out,
        out_shape=[jax.ShapeDtypeStruct((s, n_cols), BF16)] + r_shapes,
        compiler_params=_params("parallel", "parallel"),
        name="proj_" + epilogue,
    )(*args, *r_args)
    return (outs[0], *_rider_results(riders, outs[1:]))


def _gmlp_kernel(u_ref, v_ref, lng_ref, lnb_ref, ws_ref, bs_ref, o_ref):
    v = v_ref[...].astype(F32)
    mu = jnp.mean(v, axis=-1, keepdims=True)
    xc = v - mu
    var = jnp.mean(xc * xc, axis=-1, keepdims=True)
    vn = (xc * lax.rsqrt(var + LN_EPS) * lng_ref[...] + lnb_ref[...]).astype(BF16)
    t, width = vn.shape
    fused = ws_ref.shape[2] // CHUNK
    span = fused * GM_GROUP_DIM
    zero = jnp.zeros((CHUNK, GM_GROUP_DIM), BF16)
    for c in range(t // CHUNK):
        rows = slice(c * CHUNK, (c + 1) * CHUNK)
        for p in range(width // span):
            cols = slice(p * span, (p + 1) * span)
            blocks = [vn[rows, p * span + q * GM_GROUP_DIM:p * span + (q + 1) * GM_GROUP_DIM]
                      for q in range(fused)]
            rhs = jnp.concatenate(
                [jnp.concatenate([blocks[q] if q2 == q else zero for q2 in range(fused)], axis=1)
                 for q in range(fused)], axis=0)
            mixed = jnp.dot(ws_ref[p], rhs, preferred_element_type=F32) + bs_ref[:, cols]
            o_ref[rows, cols] = (u_ref[rows, cols].astype(F32) * mixed).astype(o_ref.dtype)


GM_FUSED_GROUPS = 2


def _gmlp_operands(uv, ln_g, ln_b, w_s, b_s):
    width = uv.shape[1] // 2
    groups = w_s.shape[0]
    fused = GM_FUSED_GROUPS if groups % GM_FUSED_GROUPS == 0 else 1
    ws = jnp.concatenate([w_s[q::fused] for q in range(fused)], axis=2).astype(BF16)
    bs_full = jnp.repeat(b_s.T, GM_GROUP_DIM, axis=1)
    return [uv, uv, ln_g.reshape(1, width), ln_b.reshape(1, width), ws, bs_full]


def _gmlp_specs(t, operands, step):
    width = operands[0].shape[1] // 2
    const = lambda x: pl.BlockSpec(x.shape, lambda *idx: (0,) * x.ndim)
    in_specs = [pl.BlockSpec((t, width), lambda *idx: (step(*idx), 0)),
                pl.BlockSpec((t, width), lambda *idx: (step(*idx), 1))] + [const(x) for x in operands[2:]]
    return in_specs, pl.BlockSpec((t, width), lambda *idx: (step(*idx), 0))


def _gmlp(uv, ln_g, ln_b, w_s, b_s):
    s = uv.shape[0]
    width = uv.shape[1] // 2
    t = _pick(s, (512, 256, 128))
    operands = _gmlp_operands(uv, ln_g, ln_b, w_s, b_s)
    in_specs, out_spec = _gmlp_specs(t, operands, lambda i: i)
    return pl.pallas_call(
        _gmlp_kernel,
        grid=(s // t,),
        in_specs=in_specs,
        out_specs=out_spec,
        out_shape=jax.ShapeDtypeStruct((s, width), BF16),
        compiler_params=_params("parallel"),
        name="gmlp",
    )(*operands)


class _GmlpRider:
    def __init__(self, operands, gi, gj):
        uv = operands[0]
        s, width = uv.shape[0], uv.shape[1] // 2
        self.operands = operands
        t = s // (gi * gj)
        self.rides = t * gi * gj == s and t % CHUNK == 0
        self.args, self.in_specs, self.out_specs, self.out_shapes = [], [], [], []
        if self.rides:
            self.args = _gmlp_operands(*operands)
            self.in_specs, out_spec = _gmlp_specs(t, self.args, lambda i, j: i * gj + j)
            self.out_specs = [out_spec]
            self.out_shapes = [jax.ShapeDtypeStruct((s, width), BF16)]

    def body(self, in_refs, out_refs):
        if self.rides:
            _gmlp_kernel(*in_refs, *out_refs)

    def results(self, outs):
        return outs[0] if self.rides else _gmlp(*self.operands)


NA_MASKED_OFFSET = 2 * NA_WIN_H_MAX - 1
NA_PAIR = 2
NA_GROUP = 4


def _na_strip_plan(rows):
    kh = min(NA_WIN_H_MAX, rows)
    assert kh == NA_WIN_H_MAX and rows % NA_Q_ROWS == 0 and rows >= 2 * NA_K_ROWS
    assert NA_PAIR * GRID_W == 128 and NA_K_ROWS % NA_PAIR == 0
    starts = ((0, 0), (NA_Q_ROWS, 0), (rows - NA_Q_ROWS, rows - NA_K_ROWS))
    pairs, plan = [], []
    for r0, kb0 in starts:
        strips = []
        for rl in range(NA_Q_ROWS):
            r = r0 + rl
            rs = int(np.clip(r - kh // 2, 0, rows - kh))
            offs = [kb0 + krl - r + (NA_WIN_H_MAX - 1) if rs <= kb0 + krl < rs + kh else NA_MASKED_OFFSET
                    for krl in range(NA_K_ROWS)]
            row_pairs = [tuple(offs[j:j + NA_PAIR]) for j in range(0, NA_K_ROWS, NA_PAIR)]
            live = [j for j, pr in enumerate(row_pairs) if any(o != NA_MASKED_OFFSET for o in pr)]
            lo, hi = live[0], live[-1] + 1
            ids = []
            for pr in row_pairs[lo:hi]:
                if pr not in pairs:
                    pairs.append(pr)
                ids.append(pairs.index(pr))
            strips.append((lo, hi, tuple(ids)))
        plan.append(tuple(strips))
    return tuple(plan), np.asarray(pairs, np.int32)


def _na_offsets(qb, n_blocks):
    tq = NA_Q_ROWS * GRID_W
    kb = jnp.clip(qb - 1, 0, n_blocks - NA_K_ROWS // NA_Q_ROWS)
    return pl.multiple_of(qb * tq, tq), pl.multiple_of(kb * tq, tq)


def _na_scores(q_ref, k_ref, s_ref, qb, n_blocks):
    q0, k0 = _na_offsets(qb, n_blocks)
    q = q_ref[pl.ds(q0, NA_Q_ROWS * GRID_W), :]
    k = k_ref[pl.ds(k0, NA_K_ROWS * GRID_W), :]
    s_ref[...] = lax.dot_general(q, k, (((1,), (1,)), ((), ())), preferred_element_type=F32)


def _na_softmax_pv(s_ref, v_ref, pair_ref, o_ref, qb, n_blocks, strips):
    tq = NA_Q_ROWS * GRID_W
    tk = NA_K_ROWS * GRID_W
    lanes = NA_PAIR * GRID_W
    q0, k0 = _na_offsets(qb, n_blocks)
    v = v_ref[pl.ds(k0, tk), :]
    p_rows, l_rows = [], []
    for rl, (lo, hi, ids) in enumerate(strips):
        bias = jnp.concatenate([pair_ref[i] for i in ids], axis=1)
        s_rl = s_ref[rl * GRID_W:(rl + 1) * GRID_W, lo * lanes:hi * lanes] + bias
        m = jnp.max(s_rl, axis=-1, keepdims=True)
        p = jnp.exp(s_rl - m)
        l_rows.append(jnp.sum(p, axis=-1, keepdims=True))
        pieces = [p.astype(BF16)]
        if lo > 0:
            pieces.insert(0, jnp.zeros((GRID_W, lo * lanes), BF16))
        if hi * lanes < tk:
            pieces.append(jnp.zeros((GRID_W, tk - hi * lanes), BF16))
        p_rows.append(jnp.concatenate(pieces, axis=1))
    p = jnp.concatenate(p_rows, axis=0)
    o = jnp.dot(p, v, preferred_element_type=F32) / jnp.concatenate(l_rows, axis=0)
    o_ref[pl.ds(q0, tq), :] = o.astype(o_ref.dtype)


def _na_kernel(q_ref, k_ref, v_ref, pair_ref, o_ref, s_even, s_odd, *, n_blocks, group, plan):
    first, interior, last = plan
    n_groups = n_blocks // group
    bufs = (s_even, s_odd)

    def scores(g, buf):
        for u in range(group):
            _na_scores(q_ref, k_ref, buf.at[u], g * group + u, n_blocks)

    def finish(g, buf, strips):
        for u in range(group):
            _na_softmax_pv(buf.at[u], v_ref, pair_ref, o_ref, g * group + u, n_blocks, strips[u])

    def strips_of(g):
        strips = [interior] * group
        if g == 0:
            strips[0] = first
        if g == n_groups - 1:
            strips[-1] = last
        return strips

    def step(g, parity, strips):
        finish(g, bufs[parity], strips)
        scores(g + 1, bufs[1 - parity])

    scores(0, s_even)
    step(0, 0, strips_of(0))
    n_mid = n_groups - 2

    def body(t, carry):
        g = 1 + 2 * t
        step(g, 1, [interior] * group)
        step(g + 1, 0, [interior] * group)
        return carry

    lax.fori_loop(0, n_mid // 2, body, 0)
    if n_mid % 2:
        step(n_groups - 2, (n_groups - 2) % 2, strips_of(n_groups - 2))
    finish(n_groups - 1, bufs[(n_groups - 1) % 2], strips_of(n_groups - 1))


def _neighbourhood_attention(qk, v, rpb):
    s = v.shape[0]
    width = v.shape[1]
    heads = width // NA_HEAD_DIM
    rows = s // GRID_W
    n_blocks = rows // NA_Q_ROWS
    plan, pairs = _na_strip_plan(rows)
    c = np.arange(GRID_W)
    cs = np.clip(c - NA_WIN_W // 2, 0, GRID_W - NA_WIN_W)
    col_ok = (c[None, :] >= cs[:, None]) & (c[None, :] < cs[:, None] + NA_WIN_W)
    rel = c[None, :] - c[:, None] + (NA_WIN_W - 1)
    onehot = ((rel[None] == np.arange(2 * NA_WIN_W - 1)[:, None, None]) & col_ok[None]).astype(np.float32)
    tiles = jnp.einsum("hab,bck->hack", rpb.astype(F32), onehot, precision=lax.Precision.HIGHEST)
    tiles = jnp.where(col_ok[None, None], tiles, np.float32(MASK_VALUE))
    masked = jnp.full((heads, GRID_W, GRID_W), MASK_VALUE, F32)
    tile_of = lambda a: masked if a == NA_MASKED_OFFSET else tiles[:, a]
    pair_tiles = jnp.stack([jnp.concatenate([tile_of(int(a)) for a in pr], axis=-1) for pr in pairs],
                           axis=1)
    n_pairs = pairs.shape[0]
    group = _pick(n_blocks // 2, (NA_GROUP, 2, 1))
    s_buf = pltpu.VMEM((group, NA_Q_ROWS * GRID_W, NA_K_ROWS * GRID_W), F32)
    return pl.pallas_call(
        functools.partial(_na_kernel, n_blocks=n_blocks, group=group, plan=plan),
        grid=(heads,),
        scratch_shapes=[s_buf, s_buf],
        in_specs=[pl.BlockSpec((s, NA_HEAD_DIM), lambda h: (0, h)),
                  pl.BlockSpec((s, NA_HEAD_DIM), lambda h: (0, heads + h)),
                  pl.BlockSpec((s, NA_HEAD_DIM), lambda h: (0, h)),
                  pl.BlockSpec((None, n_pairs, GRID_W, NA_PAIR * GRID_W), lambda h: (h, 0, 0, 0))],
        out_specs=pl.BlockSpec((s, NA_HEAD_DIM), lambda h: (0, h)),
        out_shape=jax.ShapeDtypeStruct((s, width), BF16),
        compiler_params=_params("parallel"),
        name="natten",
    )(qk, qk, v, pair_tiles)


def _merge_kernel(ya_ref, yb_ref, wa_ref, wb_ref, ga_ref, gb_ref, o_ref):
    a = jnp.dot(ya_ref[...], wa_ref[...], preferred_element_type=F32)
    b = jnp.dot(yb_ref[...], wb_ref[...], preferred_element_type=F32)
    o_ref[...] = (ga_ref[...].astype(F32) * a + gb_ref[...].astype(F32) * b).astype(o_ref.dtype)


def _merge(y_a, y_b, w_oa, w_ob, sg):
    s, k = y_a.shape
    n = w_oa.shape[1]
    tm = _pick(s, (1024, 512, 256, 128))
    tn = _pick(n, (1024, 512, 256, 128))
    gb_off = n // tn
    return pl.pallas_call(
        _merge_kernel,
        grid=(s // tm, n // tn),
        in_specs=[pl.BlockSpec((tm, k), lambda i, j: (i, 0)),
                  pl.BlockSpec((tm, k), lambda i, j: (i, 0)),
                  pl.BlockSpec((k, tn), lambda i, j: (0, j)),
                  pl.BlockSpec((k, tn), lambda i, j: (0, j)),
                  pl.BlockSpec((tm, tn), lambda i, j: (i, j)),
                  pl.BlockSpec((tm, tn), lambda i, j: (i, j + gb_off))],
        out_specs=pl.BlockSpec((tm, tn), lambda i, j: (i, j)),
        out_shape=jax.ShapeDtypeStruct((s, n), BF16),
        compiler_params=_params("parallel", "parallel"),
        name="merge",
    )(y_a, y_b, w_oa, w_ob, sg, sg)


def _matmul_residual_kernel(x_ref, w_ref, r_ref, o_ref):
    o_ref[...] = r_ref[...] + jnp.dot(x_ref[...], w_ref[...], preferred_element_type=F32)


def _matmul_residual(lhs, w, residual, tm_prefs, tn_prefs, name):
    s, k = lhs.shape
    n = w.shape[1]
    tm = _pick(s, tm_prefs)
    tn = _pick(n, tn_prefs)
    return pl.pallas_call(
        _matmul_residual_kernel,
        grid=(s // tm, n // tn),
        in_specs=[pl.BlockSpec((tm, k), lambda i, j: (i, 0)),
                  pl.BlockSpec((k, tn), lambda i, j: (0, j)),
                  pl.BlockSpec((tm, tn), lambda i, j: (i, j))],
        out_specs=pl.BlockSpec((tm, tn), lambda i, j: (i, j)),
        out_shape=jax.ShapeDtypeStruct((s, n), F32),
        compiler_params=_params("parallel", "parallel"),
        name=name,
    )(lhs, w, residual)


def _matmul_residual_wide_k(lhs, w, residual, name):
    s, k = lhs.shape
    n = w.shape[1]
    tm = _pick(s, (512, 256, 128))
    tn = _pick(n, (1024, 512, 256, 128))
    return pl.pallas_call(
        _matmul_residual_kernel,
        grid=(n // tn, s // tm),
        in_specs=[pl.BlockSpec((tm, k), lambda j, i: (i, 0)),
                  pl.BlockSpec((k, tn), lambda j, i: (0, j), pipeline_mode=pl.Buffered(1)),
                  pl.BlockSpec((tm, tn), lambda j, i: (i, j))],
        out_specs=pl.BlockSpec((tm, tn), lambda j, i: (i, j)),
        out_shape=jax.ShapeDtypeStruct((s, n), F32),
        compiler_params=_params("parallel", "parallel"),
        name=name,
    )(lhs, w, residual)


def _swiglu_kernel(x_ref, wg_ref, wu_ref, *rest, riders):
    _, o_ref, rider_in, rider_out = _split_refs(rest, riders)
    x = x_ref[...]
    g = jnp.dot(x, wg_ref[...], preferred_element_type=F32)
    u = jnp.dot(x, wu_ref[...], preferred_element_type=F32)
    o_ref[...] = (g * _sigmoid(g) * u).astype(o_ref.dtype)
    _run_riders(riders, rider_in, rider_out)


def _swiglu_up(hn, w_gate, w_up, riders=()):
    s, d = hn.shape
    f = w_gate.shape[1]
    tm = _pick(s, (4096, 2048, 1024, 512, 256, 128))
    tn = _pick(f, (256, 128))
    gi, gj = s // tm, f // tn
    riders = tuple(make(gi, gj) for make in riders)
    r_args, r_in, r_out, r_shapes = _rider_specs(riders)
    outs = pl.pallas_call(
        functools.partial(_swiglu_kernel, riders=riders),
        grid=(gi, gj),
        in_specs=[pl.BlockSpec((tm, d), lambda i, j: (i, 0), pipeline_mode=pl.Buffered(1)),
                  pl.BlockSpec((d, tn), lambda i, j: (0, j)),
                  pl.BlockSpec((d, tn), lambda i, j: (0, j))] + r_in,
        out_specs=[pl.BlockSpec((tm, tn), lambda i, j: (i, j))] + r_out,
        out_shape=[jax.ShapeDtypeStruct((s, f), BF16)] + r_shapes,
        compiler_params=_params("parallel", "parallel"),
        name="swiglu_up",
    )(hn, w_gate, w_up, *r_args)
    return (outs[0], *_rider_results(riders, outs[1:]))


def kernel(x, norm1_g, w_in, gm_ln_g, gm_ln_b, gm_w_s, gm_b_s, q_gain, k_gain, na_rpb,
           w_o_gm, w_o_na, w_out, norm2_g, w_ff_gate, w_ff_up, w_ff_down):
    batch, seq, d_model = x.shape
    depth = w_in.shape[0]
    gm_width = gm_ln_g.shape[1]
    na_width = w_o_na.shape[1]
    heads = na_width // NA_HEAD_DIM
    outs = []
    for bi in range(batch):
        h = x.reshape(seq, d_model) if batch == 1 else x[bi]
        for layer in range(depth):
            xn = _rmsnorm(h, norm1_g[layer])
            w_uv = w_in[layer][:, :2 * gm_width].astype(BF16)
            cast = lambda *ws: functools.partial(_CastRider, ws)
            uv, (w_in_l,) = _project(xn, w_uv, 0, 2 * gm_width, "gelu", riders=(cast(w_in[layer]),))
            gains = jnp.concatenate([jnp.tile(q_gain[layer] * np.float32(NA_HEAD_DIM ** -0.5), heads),
                                     jnp.tile(k_gain[layer], heads)]).reshape(1, 2 * na_width)
            qk, (w_oa, w_ob, w_o) = _project(xn, w_in_l, 2 * gm_width, 2 * na_width, "headnorm", gains,
                                             riders=(cast(w_o_gm[layer], w_o_na[layer], w_out[layer]),))
            v, = _project(xn, w_in_l, 2 * gm_width + 2 * na_width, na_width, "none")
            gmlp = functools.partial(
                _GmlpRider, (uv, gm_ln_g[layer], gm_ln_b[layer], gm_w_s[layer], gm_b_s[layer]))
            sg, (w_gate, w_up), y_a = _project(
                xn, w_in_l, 2 * gm_width + 3 * na_width, 2 * d_model, "sigmoid",
                riders=(cast(w_ff_gate[layer], w_ff_up[layer]), gmlp))
            y_b = _neighbourhood_attention(qk, v, na_rpb[layer])
            merged = _merge(y_a, y_b, w_oa, w_ob, sg)
            h = _matmul_residual(merged, w_o, h,
                                 (1024, 512, 256, 128), (1024, 512, 256, 128), "out_proj")
            hn = _rmsnorm(h, norm2_g[layer])
            act, (w_down,) = _swiglu_up(hn, w_gate, w_up, riders=(cast(w_ff_down[layer]),))
            h = _matmul_residual_wide_k(act, w_down, h, "ffn_down")
        outs.append(h)
    if batch == 1:
        return outs[0].reshape(1, seq, d_model)
    return jnp.stack(outs)
```

```python
import functools

import numpy as np
import jax
import jax.numpy as jnp
from jax import lax
from jax.experimental import pallas as pl
from jax.experimental.pallas import tpu as pltpu

GRID_W = 64
CHUNK = 128
GM_GROUP_DIM = 128
NA_HEAD_DIM = 128
NA_WIN_H_MAX = 8
NA_WIN_W = 16
RMS_EPS = 1e-6
LN_EPS = 1e-5

V7X_VMEM_BYTES = 64 * 1024 * 1024
VMEM_LIMIT_BYTES = V7X_VMEM_BYTES - 6 * 1024 * 1024

NA_Q_ROWS = 4
NA_K_ROWS = NA_Q_ROWS + NA_WIN_H_MAX
MASK_VALUE = -1e30

F32 = jnp.float32
BF16 = jnp.bfloat16


def _params(*sem):
    return pltpu.CompilerParams(dimension_semantics=sem, vmem_limit_bytes=VMEM_LIMIT_BYTES)


def _pick(n, prefs):
    for p in prefs:
        if n % p == 0:
            return p
    return n


def _rmsnorm_kernel(x_ref, g_ref, o_ref):
    x = x_ref[...]
    ms = jnp.mean(x * x, axis=-1, keepdims=True)
    o_ref[...] = (x * lax.rsqrt(ms + RMS_EPS) * g_ref[...]).astype(o_ref.dtype)


def _rmsnorm(x, g):
    s, d = x.shape
    tr = _pick(s, (512, 256, 128))
    return pl.pallas_call(
        _rmsnorm_kernel,
        grid=(s // tr,),
        in_specs=[pl.BlockSpec((tr, d), lambda i: (i, 0)),
                  pl.BlockSpec((1, d), lambda i: (0, 0))],
        out_specs=pl.BlockSpec((tr, d), lambda i: (i, 0)),
        out_shape=jax.ShapeDtypeStruct((s, d), BF16),
        compiler_params=_params("parallel"),
        name="rmsnorm",
    )(x, g.reshape(1, d))


def _gelu(x):
    return 0.5 * x * (1.0 + lax.erf(x * np.float32(np.sqrt(0.5))))


def _sigmoid(x):
    return 0.5 * (jnp.tanh(0.5 * x) + 1.0)


BF16_SUBLANES = 16
LANES = 128


class _CastRider:
    def __init__(self, weights, gi, gj):
        steps = gi * gj
        self.weights = list(weights)
        self.rides = [w.shape[0] % (steps * BF16_SUBLANES) == 0 for w in self.weights]
        self.args = [w for w, r in zip(self.weights, self.rides) if r]
        self.in_specs = [pl.BlockSpec((w.shape[0] // steps, w.shape[1]), lambda i, j: (i * gj + j, 0))
                         for w in self.args]
        self.out_specs = list(self.in_specs)
        self.out_shapes = [jax.ShapeDtypeStruct(w.shape, BF16) for w in self.args]

    def body(self, in_refs, out_refs):
        for src, dst in zip(in_refs, out_refs):
            dst[...] = src[...].astype(dst.dtype)

    def results(self, outs):
        outs = list(outs)
        return [outs.pop(0) if r else w.astype(BF16) for w, r in zip(self.weights, self.rides)]


def _rider_specs(riders):
    cat = lambda name: [x for r in riders for x in getattr(r, name)]
    return cat("args"), cat("in_specs"), cat("out_specs"), cat("out_shapes")


def _split_refs(rest, riders):
    n_in = sum(len(r.in_specs) for r in riders)
    n_out = sum(len(r.out_specs) for r in riders)
    own = rest[:len(rest) - n_in - 1 - n_out]
    return own, rest[len(own) + n_in], rest[len(own):len(own) + n_in], rest[len(own) + n_in + 1:]


def _run_riders(riders, in_refs, out_refs):
    for r in riders:
        n_in, n_out = len(r.in_specs), len(r.out_specs)
        r.body(in_refs[:n_in], out_refs[:n_out])
        in_refs, out_refs = in_refs[n_in:], out_refs[n_out:]


def _rider_results(riders, outs):
    outs, res = list(outs), []
    for r in riders:
        n_out = len(r.out_specs)
        res.append(r.results(outs[:n_out]))
        outs = outs[n_out:]
    return res


def _proj_kernel(x_ref, w_ref, *rest, epilogue, riders):
    rest, o_ref, rider_in, rider_out = _split_refs(rest, riders)
    acc = jnp.dot(x_ref[...], w_ref[...], preferred_element_type=F32)
    if epilogue == "gelu":
        out = _gelu(acc)
    elif epilogue == "sigmoid":
        out = _sigmoid(acc)
    elif epilogue == "headnorm":
        gain_ref = rest[0]
        parts = []
        for h in range(acc.shape[1] // NA_HEAD_DIM):
            blk = acc[:, h * NA_HEAD_DIM:(h + 1) * NA_HEAD_DIM]
            ms = jnp.mean(blk * blk, axis=-1, keepdims=True)
            parts.append(blk * lax.rsqrt(ms + RMS_EPS))
        out = jnp.concatenate(parts, axis=1) * gain_ref[...]
    else:
        out = acc
    o_ref[...] = out.astype(o_ref.dtype)
    _run_riders(riders, rider_in, rider_out)


def _project(xn, w, col_start, n_cols, epilogue, gain=None, riders=()):
    s, d = xn.shape
    tm = _pick(s, (1024, 512, 256, 128))
    tn = _pick(int(np.gcd(n_cols, col_start)), (1024, 512, 256, 128))
    off = col_start // tn
    gi, gj = s // tm, n_cols // tn
    in_specs = [pl.BlockSpec((tm, d), lambda i, j: (i, 0)),
                pl.BlockSpec((d, tn), lambda i, j: (0, j + off))]
    args = [xn, w]
    if gain is not None:
        in_specs.append(pl.BlockSpec((1, tn), lambda i, j: (0, j)))
        args.append(gain)
    riders = tuple(make(gi, gj) for make in riders)
    r_args, r_in, r_out, r_shapes = _rider_specs(riders)
    outs = pl.pallas_call(
        functools.partial(_proj_kernel, epilogue=epilogue, riders=riders),
        grid=(gi, gj),
        in_specs=in_specs + r_in,
        out_specs=[pl.BlockSpec((tm, tn), lambda i, j: (i, j))] + r_out,
        out_shape=[jax.ShapeDtypeStruct((s, n_cols), BF16)] + r_shapes,
        compiler_params=_params("parallel", "parallel"),
        name="proj_" + epilogue,
    )(*args, *r_args)
    return (outs[0], *_rider_results(riders, outs[1:]))


def _gmlp_kernel(u_ref, v_ref, lng_ref, lnb_ref, ws_ref, bs_ref, o_ref):
    v = v_ref[...].astype(F32)
    mu = jnp.mean(v, axis=-1, keepdims=True)
    xc = v - mu
    var = jnp.mean(xc * xc, axis=-1, keepdims=True)
    vn = (xc * lax.rsqrt(var + LN_EPS) * lng_ref[...] + lnb_ref[...]).astype(BF16)
    t, width = vn.shape
    fused = ws_ref.shape[2] // CHUNK
    span = fused * GM_GROUP_DIM
    zero = jnp.zeros((CHUNK, GM_GROUP_DIM), BF16)
    for c in range(t // CHUNK):
        rows = slice(c * CHUNK, (c + 1) * CHUNK)
        for p in range(width // span):
            cols = slice(p * span, (p + 1) * span)
            blocks = [vn[rows, p * span + q * GM_GROUP_DIM:p * span + (q + 1) * GM_GROUP_DIM]
                      for q in range(fused)]
            rhs = jnp.concatenate(
                [jnp.concatenate([blocks[q] if q2 == q else zero for q2 in range(fused)], axis=1)
                 for q in range(fused)], axis=0)
            mixed = jnp.dot(ws_ref[p], rhs, preferred_element_type=F32) + bs_ref[:, cols]
            o_ref[rows, cols] = (u_ref[rows, cols].astype(F32) * mixed).astype(o_ref.dtype)


GM_FUSED_GROUPS = 2


def _gmlp_operands(uv, ln_g, ln_b, w_s, b_s):
    width = uv.shape[1] // 2
    groups = w_s.shape[0]
    fused = GM_FUSED_GROUPS if groups % GM_FUSED_GROUPS == 0 else 1
    ws = jnp.concatenate([w_s[q::fused] for q in range(fused)], axis=2).astype(BF16)
    bs_full = jnp.repeat(b_s.T, GM_GROUP_DIM, axis=1)
    return [uv, uv, ln_g.reshape(1, width), ln_b.reshape(1, width), ws, bs_full]


def _gmlp_specs(t, operands, step):
    width = operands[0].shape[1] // 2
    const = lambda x: pl.BlockSpec(x.shape, lambda *idx: (0,) * x.ndim)
    in_specs = [pl.BlockSpec((t, width), lambda *idx: (step(*idx), 0)),
                pl.BlockSpec((t, width), lambda *idx: (step(*idx), 1))] + [const(x) for x in operands[2:]]
    return in_specs, pl.BlockSpec((t, width), lambda *idx: (step(*idx), 0))


def _gmlp(uv, ln_g, ln_b, w_s, b_s):
    s = uv.shape[0]
    width = uv.shape[1] // 2
    t = _pick(s, (512, 256, 128))
    operands = _gmlp_operands(uv, ln_g, ln_b, w_s, b_s)
    in_specs, out_spec = _gmlp_specs(t, operands, lambda i: i)
    return pl.pallas_call(
        _gmlp_kernel,
        grid=(s // t,),
        in_specs=in_specs,
        out_specs=out_spec,
        out_shape=jax.ShapeDtypeStruct((s, width), BF16),
        compiler_params=_params("parallel"),
        name="gmlp",
    )(*operands)


class _GmlpRider:
    def __init__(self, operands, gi, gj):
        uv = operands[0]
        s, width = uv.shape[0], uv.shape[1] // 2
        self.operands = operands
        t = s // (gi * gj)
        self.rides = t * gi * gj == s and t % CHUNK == 0
        self.args, self.in_specs, self.out_specs, self.out_shapes = [], [], [], []
        if self.rides:
            self.args = _gmlp_operands(*operands)
            self.in_specs, out_spec = _gmlp_specs(t, self.args, lambda i, j: i * gj + j)
            self.out_specs = [out_spec]
            self.out_shapes = [jax.ShapeDtypeStruct((s, width), BF16)]

    def body(self, in_refs, out_refs):
        if self.rides:
            _gmlp_kernel(*in_refs, *out_refs)

    def results(self, outs):
        return outs[0] if self.rides else _gmlp(*self.operands)


NA_MASKED_OFFSET = 2 * NA_WIN_H_MAX - 1
NA_PAIR = 2
NA_GROUP = 4


def _na_strip_plan(rows):
    kh = min(NA_WIN_H_MAX, rows)
    assert kh == NA_WIN_H_MAX and rows % NA_Q_ROWS == 0 and rows >= 2 * NA_K_ROWS
    assert NA_PAIR * GRID_W == 128 and NA_K_ROWS % NA_PAIR == 0
    starts = ((0, 0), (NA_Q_ROWS, 0), (rows - NA_Q_ROWS, rows - NA_K_ROWS))
    pairs, plan = [], []
    for r0, kb0 in starts:
        strips = []
        for rl in range(NA_Q_ROWS):
            r = r0 + rl
            rs = int(np.clip(r - kh // 2, 0, rows - kh))
            offs = [kb0 + krl - r + (NA_WIN_H_MAX - 1) if rs <= kb0 + krl < rs + kh else NA_MASKED_OFFSET
                    for krl in range(NA_K_ROWS)]
            row_pairs = [tuple(offs[j:j + NA_PAIR]) for j in range(0, NA_K_ROWS, NA_PAIR)]
            live = [j for j, pr in enumerate(row_pairs) if any(o != NA_MASKED_OFFSET for o in pr)]
            lo, hi = live[0], live[-1] + 1
            ids = []
            for pr in row_pairs[lo:hi]:
                if pr not in pairs:
                    pairs.append(pr)
                ids.append(pairs.index(pr))
            strips.append((lo, hi, tuple(ids)))
        plan.append(tuple(strips))
    return tuple(plan), np.asarray(pairs, np.int32)


def _na_offsets(qb, n_blocks):
    tq = NA_Q_ROWS * GRID_W
    kb = jnp.clip(qb - 1, 0, n_blocks - NA_K_ROWS // NA_Q_ROWS)
    return pl.multiple_of(qb * tq, tq), pl.multiple_of(kb * tq, tq)


def _na_scores(q_ref, k_ref, s_ref, qb, n_blocks):
    q0, k0 = _na_offsets(qb, n_blocks)
    q = q_ref[pl.ds(q0, NA_Q_ROWS * GRID_W), :]
    k = k_ref[pl.ds(k0, NA_K_ROWS * GRID_W), :]
    s_ref[...] = lax.dot_general(q, k, (((1,), (1,)), ((), ())), preferred_element_type=F32)


def _na_softmax_pv(s_ref, v_ref, pair_ref, o_ref, qb, n_blocks, strips):
    tq = NA_Q_ROWS * GRID_W
    tk = NA_K_ROWS * GRID_W
    lanes = NA_PAIR * GRID_W
    q0, k0 = _na_offsets(qb, n_blocks)
    v = v_ref[pl.ds(k0, tk), :]
    p_rows, l_rows = [], []
    for rl, (lo, hi, ids) in enumerate(strips):
        bias = jnp.concatenate([pair_ref[i] for i in ids], axis=1)
        s_rl = s_ref[rl * GRID_W:(rl + 1) * GRID_W, lo * lanes:hi * lanes] + bias
        m = jnp.max(s_rl, axis=-1, keepdims=True)
        p = jnp.exp(s_rl - m)
        l_rows.append(jnp.sum(p, axis=-1, keepdims=True))
        pieces = [p.astype(BF16)]
        if lo > 0:
            pieces.insert(0, jnp.zeros((GRID_W, lo * lanes), BF16))
        if hi * lanes < tk:
            pieces.append(jnp.zeros((GRID_W, tk - hi * lanes), BF16))
        p_rows.append(jnp.concatenate(pieces, axis=1))
    p = jnp.concatenate(p_rows, axis=0)
    o = jnp.dot(p, v, preferred_element_type=F32) / jnp.concatenate(l_rows, axis=0)
    o_ref[pl.ds(q0, tq), :] = o.astype(o_ref.dtype)


def _na_kernel(q_ref, k_ref, v_ref, pair_ref, o_ref, s_even, s_odd, *, n_blocks, group, plan):
    first, interior, last = plan
    n_groups = n_blocks // group
    bufs = (s_even, s_odd)

    def scores(g, buf):
        for u in range(group):
            _na_scores(q_ref, k_ref, buf.at[u], g * group + u, n_blocks)

    def finish(g, buf, strips):
        for u in range(group):
            _na_softmax_pv(buf.at[u], v_ref, pair_ref, o_ref, g * group + u, n_blocks, strips[u])

    def strips_of(g):
        strips = [interior] * group
        if g == 0:
            strips[0] = first
        if g == n_groups - 1:
            strips[-1] = last
        return strips

    def step(g, parity, strips):
        finish(g, bufs[parity], strips)
        scores(g + 1, bufs[1 - parity])

    scores(0, s_even)
    step(0, 0, strips_of(0))
    n_mid = n_groups - 2

    def body(t, carry):
        g = 1 + 2 * t
        step(g, 1, [interior] * group)
        step(g + 1, 0, [interior] * group)
        return carry

    lax.fori_loop(0, n_mid // 2, body, 0)
    if n_mid % 2:
        step(n_groups - 2, (n_groups - 2) % 2, strips_of(n_groups - 2))
    finish(n_groups - 1, bufs[(n_groups - 1) % 2], strips_of(n_groups - 1))


def _neighbourhood_attention(qk, v, rpb):
    s = v.shape[0]
    width = v.shape[1]
    heads = width // NA_HEAD_DIM
    rows = s // GRID_W
    n_blocks = rows // NA_Q_ROWS
    plan, pairs = _na_strip_plan(rows)
    c = np.arange(GRID_W)
    cs = np.clip(c - NA_WIN_W // 2, 0, GRID_W - NA_WIN_W)
    col_ok = (c[None, :] >= cs[:, None]) & (c[None, :] < cs[:, None] + NA_WIN_W)
    rel = c[None, :] - c[:, None] + (NA_WIN_W - 1)
    onehot = ((rel[None] == np.arange(2 * NA_WIN_W - 1)[:, None, None]) & col_ok[None]).astype(np.float32)
    tiles = jnp.einsum("hab,bck->hack", rpb.astype(F32), onehot, precision=lax.Precision.HIGHEST)
    tiles = jnp.where(col_ok[None, None], tiles, np.float32(MASK_VALUE))
    masked = jnp.full((heads, GRID_W, GRID_W), MASK_VALUE, F32)
    tile_of = lambda a: masked if a == NA_MASKED_OFFSET else tiles[:, a]
    pair_tiles = jnp.stack([jnp.concatenate([tile_of(int(a)) for a in pr], axis=-1) for pr in pairs],
                           axis=1)
    n_pairs = pairs.shape[0]
    group = _pick(n_blocks // 2, (NA_GROUP, 2, 1))
    s_buf = pltpu.VMEM((group, NA_Q_ROWS * GRID_W, NA_K_ROWS * GRID_W), F32)
    return pl.pallas_call(
        functools.partial(_na_kernel, n_blocks=n_blocks, group=group, plan=plan),
        grid=(heads,),
        scratch_shapes=[s_buf, s_buf],
        in_specs=[pl.BlockSpec((s, NA_HEAD_DIM), lambda h: (0, h)),
                  pl.BlockSpec((s, NA_HEAD_DIM), lambda h: (0, heads + h)),
                  pl.BlockSpec((s, NA_HEAD_DIM), lambda h: (0, h)),
                  pl.BlockSpec((None, n_pairs, GRID_W, NA_PAIR * GRID_W), lambda h: (h, 0, 0, 0))],
        out_specs=pl.BlockSpec((s, NA_HEAD_DIM), lambda h: (0, h)),
        out_shape=jax.ShapeDtypeStruct((s, width), BF16),
        compiler_params=_params("parallel"),
        name="natten",
    )(qk, qk, v, pair_tiles)


def _merge_kernel(ya_ref, yb_ref, wa_ref, wb_ref, ga_ref, gb_ref, o_ref):
    a = jnp.dot(ya_ref[...], wa_ref[...], preferred_element_type=F32)
    b = jnp.dot(yb_ref[...], wb_ref[...], preferred_element_type=F32)
    o_ref[...] = (ga_ref[...].astype(F32) * a + gb_ref[...].astype(F32) * b).astype(o_ref.dtype)


def _merge(y_a, y_b, w_oa, w_ob, sg):
    s, k = y_a.shape
    n = w_oa.shape[1]
    tm = _pick(s, (1024, 512, 256, 128))
    tn = _pick(n, (1024, 512, 256, 128))
    gb_off = n // tn
    return pl.pallas_call(
        _merge_kernel,
        grid=(s // tm, n // tn),
        in_specs=[pl.BlockSpec((tm, k), lambda i, j: (i, 0)),
                  pl.BlockSpec((tm, k), lambda i, j: (i, 0)),
                  pl.BlockSpec((k, tn), lambda i, j: (0, j)),
                  pl.BlockSpec((k, tn), lambda i, j: (0, j)),
                  pl.BlockSpec((tm, tn), lambda i, j: (i, j)),
                  pl.BlockSpec((tm, tn), lambda i, j: (i, j + gb_off))],
        out_specs=pl.BlockSpec((tm, tn), lambda i, j: (i, j)),
        out_shape=jax.ShapeDtypeStruct((s, n), BF16),
        compiler_params=_params("parallel", "parallel"),
        name="merge",
    )(y_a, y_b, w_oa, w_ob, sg, sg)


def _matmul_residual_kernel(x_ref, w_ref, r_ref, o_ref):
    o_ref[...] = r_ref[...] + jnp.dot(x_ref[...], w_ref[...], preferred_element_type=F32)


def _matmul_residual(lhs, w, residual, tm_prefs, tn_prefs, name):
    s, k = lhs.shape
    n = w.shape[1]
    tm = _pick(s, tm_prefs)
    tn = _pick(n, tn_prefs)
    return pl.pallas_call(
        _matmul_residual_kernel,
        grid=(s // tm, n // tn),
        in_specs=[pl.BlockSpec((tm, k), lambda i, j: (i, 0)),
                  pl.BlockSpec((k, tn), lambda i, j: (0, j)),
                  pl.BlockSpec((tm, tn), lambda i, j: (i, j))],
        out_specs=pl.BlockSpec((tm, tn), lambda i, j: (i, j)),
        out_shape=jax.ShapeDtypeStruct((s, n), F32),
        compiler_params=_params("parallel", "parallel"),
        name=name,
    )(lhs, w, residual)


NORM_ROWS = 16


def _out_proj_norm_kernel(x_ref, w_ref, r_ref, g_ref, o_ref, hn_ref, slab_ref, ss_ref, *, d_model):
    j = pl.program_id(1)
    nj = pl.num_programs(1)
    tn = o_ref.shape[1]

    @pl.when(j == 0)
    def _():
        ss_ref[...] = jnp.zeros_like(ss_ref)

    h = r_ref[...] + jnp.dot(x_ref[...], w_ref[...], preferred_element_type=F32)
    o_ref[...] = h
    slab_ref[j] = h
    sq = h * h
    part = sq[:, :LANES]
    for c in range(1, tn // LANES):
        part = part + sq[:, c * LANES:(c + 1) * LANES]
    ss_ref[...] += part

    @pl.when(j == nj - 1)
    def _():
        ss = jnp.sum(ss_ref[...], axis=-1, keepdims=True)
        inv_rms = lax.rsqrt(ss * np.float32(1.0 / d_model) + RMS_EPS)
        ss_ref[...] = jnp.broadcast_to(inv_rms, ss_ref.shape)

        def rows_chunk(r, carry):
            r0 = pl.multiple_of(r * NORM_ROWS, NORM_ROWS)
            inv = jnp.concatenate([ss_ref[pl.ds(r0, NORM_ROWS), :]] * (tn // LANES), axis=1)
            for jj in range(slab_ref.shape[0]):
                cols = slice(jj * tn, (jj + 1) * tn)
                y = slab_ref[jj, pl.ds(r0, NORM_ROWS), :] * inv * g_ref[:, cols]
                hn_ref[pl.ds(r0, NORM_ROWS), cols] = y.astype(hn_ref.dtype)
            return carry

        lax.fori_loop(0, ss_ref.shape[0] // NORM_ROWS, rows_chunk, 0)


def _out_proj_norm(merged, w, residual, gain):
    s, k = merged.shape
    n = w.shape[1]
    tm = _pick(s, (512, 256, 128))
    tn = _pick(n, (1024, 512, 256, 128))
    return pl.pallas_call(
        functools.partial(_out_proj_norm_kernel, d_model=n),
        grid=(s // tm, n // tn),
        in_specs=[pl.BlockSpec((tm, k), lambda i, j: (i, 0)),
                  pl.BlockSpec((k, tn), lambda i, j: (0, j)),
                  pl.BlockSpec((tm, tn), lambda i, j: (i, j)),
                  pl.BlockSpec((1, n), lambda i, j: (0, 0))],
        out_specs=[pl.BlockSpec((tm, tn), lambda i, j: (i, j)),
                   pl.BlockSpec((tm, n), lambda i, j: (i, 0))],
        out_shape=[jax.ShapeDtypeStruct((s, n), F32),
                   jax.ShapeDtypeStruct((s, n), BF16)],
        scratch_shapes=[pltpu.VMEM((n // tn, tm, tn), F32), pltpu.VMEM((tm, LANES), F32)],
        compiler_params=_params("parallel", "arbitrary"),
        name="out_proj",
    )(merged, w, residual, gain.reshape(1, n))


def _matmul_residual_wide_k(lhs, w, residual, name):
    s, k = lhs.shape
    n = w.shape[1]
    tm = _pick(s, (512, 256, 128))
    tn = _pick(n, (1024, 512, 256, 128))
    return pl.pallas_call(
        _matmul_residual_kernel,
        grid=(n // tn, s // tm),
        in_specs=[pl.BlockSpec((tm, k), lambda j, i: (i, 0)),
                  pl.BlockSpec((k, tn), lambda j, i: (0, j), pipeline_mode=pl.Buffered(1)),
                  pl.BlockSpec((tm, tn), lambda j, i: (i, j))],
        out_specs=pl.BlockSpec((tm, tn), lambda j, i: (i, j)),
        out_shape=jax.ShapeDtypeStruct((s, n), F32),
        compiler_params=_params("parallel", "parallel"),
        name=name,
    )(lhs, w, residual)


def _swiglu_kernel(x_ref, wg_ref, wu_ref, *rest, riders):
    _, o_ref, rider_in, rider_out = _split_refs(rest, riders)
    x = x_ref[...]
    g = jnp.dot(x, wg_ref[...], preferred_element_type=F32)
    u = jnp.dot(x, wu_ref[...], preferred_element_type=F32)
    o_ref[...] = (g * _sigmoid(g) * u).astype(o_ref.dtype)
    _run_riders(riders, rider_in, rider_out)


def _swiglu_up(hn, w_gate, w_up, riders=()):
    s, d = hn.shape
    f = w_gate.shape[1]
    tm = _pick(s, (4096, 2048, 1024, 512, 256, 128))
    tn = _pick(f, (256, 128))
    gi, gj = s // tm, f // tn
    riders = tuple(make(gi, gj) for make in riders)
    r_args, r_in, r_out, r_shapes = _rider_specs(riders)
    outs = pl.pallas_call(
        functools.partial(_swiglu_kernel, riders=riders),
        grid=(gi, gj),
        in_specs=[pl.BlockSpec((tm, d), lambda i, j: (i, 0), pipeline_mode=pl.Buffered(1)),
                  pl.BlockSpec((d, tn), lambda i, j: (0, j)),
                  pl.BlockSpec((d, tn), lambda i, j: (0, j))] + r_in,
        out_specs=[pl.BlockSpec((tm, tn), lambda i, j: (i, j))] + r_out,
        out_shape=[jax.ShapeDtypeStruct((s, f), BF16)] + r_shapes,
        compiler_params=_params("parallel", "parallel"),
        name="swiglu_up",
    )(hn, w_gate, w_up, *r_args)
    return (outs[0], *_rider_results(riders, outs[1:]))


def kernel(x, norm1_g, w_in, gm_ln_g, gm_ln_b, gm_w_s, gm_b_s, q_gain, k_gain, na_rpb,
           w_o_gm, w_o_na, w_out, norm2_g, w_ff_gate, w_ff_up, w_ff_down):
    batch, seq, d_model = x.shape
    depth = w_in.shape[0]
    gm_width = gm_ln_g.shape[1]
    na_width = w_o_na.shape[1]
    heads = na_width // NA_HEAD_DIM
    outs = []
    for bi in range(batch):
        h = x.reshape(seq, d_model) if batch == 1 else x[bi]
        for layer in range(depth):
            xn = _rmsnorm(h, norm1_g[layer])
            w_uv = w_in[layer][:, :2 * gm_width].astype(BF16)
            cast = lambda *ws: functools.partial(_CastRider, ws)
            uv, (w_in_l,) = _project(xn, w_uv, 0, 2 * gm_width, "gelu", riders=(cast(w_in[layer]),))
            gains = jnp.concatenate([jnp.tile(q_gain[layer] * np.float32(NA_HEAD_DIM ** -0.5), heads),
                                     jnp.tile(k_gain[layer], heads)]).reshape(1, 2 * na_width)
            qk, (w_oa, w_ob, w_o) = _project(xn, w_in_l, 2 * gm_width, 2 * na_width, "headnorm", gains,
                                             riders=(cast(w_o_gm[layer], w_o_na[layer], w_out[layer]),))
            v, = _project(xn, w_in_l, 2 * gm_width + 2 * na_width, na_width, "none")
            gmlp = functools.partial(
                _GmlpRider, (uv, gm_ln_g[layer], gm_ln_b[layer], gm_w_s[layer], gm_b_s[layer]))
            sg, (w_gate, w_up), y_a = _project(
                xn, w_in_l, 2 * gm_width + 3 * na_width, 2 * d_model, "sigmoid",
                riders=(cast(w_ff_gate[layer], w_ff_up[layer]), gmlp))
            y_b = _neighbourhood_attention(qk, v, na_rpb[layer])
            merged = _merge(y_a, y_b, w_oa, w_ob, sg)
            h, hn = _out_proj_norm(merged, w_o, h, norm2_g[layer])
            act, (w_down,) = _swiglu_up(hn, w_gate, w_up, riders=(cast(w_ff_down[layer]),))
            h = _matmul_residual_wide_k(act, w_down, h, "ffn_down")
        outs.append(h)
    if batch == 1:
        return outs[0].reshape(1, seq, d_model)
    return jnp.stack(outs)
```

```python
import functools

import numpy as np
import jax
import jax.numpy as jnp
from jax import lax
from jax.experimental import pallas as pl
from jax.experimental.pallas import tpu as pltpu

GRID_W = 64
CHUNK = 128
GM_GROUP_DIM = 128
NA_HEAD_DIM = 128
NA_WIN_H_MAX = 8
NA_WIN_W = 16
RMS_EPS = 1e-6
LN_EPS = 1e-5

V7X_VMEM_BYTES = 64 * 1024 * 1024
VMEM_LIMIT_BYTES = V7X_VMEM_BYTES - 6 * 1024 * 1024

NA_Q_ROWS = 4
NA_K_ROWS = NA_Q_ROWS + NA_WIN_H_MAX
MASK_VALUE = -1e30

F32 = jnp.float32
BF16 = jnp.bfloat16


def _params(*sem):
    return pltpu.CompilerParams(dimension_semantics=sem, vmem_limit_bytes=VMEM_LIMIT_BYTES)


def _pick(n, prefs):
    for p in prefs:
        if n % p == 0:
            return p
    return n


def _rmsnorm_kernel(x_ref, g_ref, o_ref):
    x = x_ref[...]
    ms = jnp.mean(x * x, axis=-1, keepdims=True)
    o_ref[...] = (x * lax.rsqrt(ms + RMS_EPS) * g_ref[...]).astype(o_ref.dtype)


def _rmsnorm(x, g, rows=None):
    d = x.shape[1]
    s = x.shape[0] if rows is None else rows
    tr = _pick(s, (512, 256, 128))
    return pl.pallas_call(
        _rmsnorm_kernel,
        grid=(s // tr,),
        in_specs=[pl.BlockSpec((tr, d), lambda i: (i, 0)),
                  pl.BlockSpec((1, d), lambda i: (0, 0))],
        out_specs=pl.BlockSpec((tr, d), lambda i: (i, 0)),
        out_shape=jax.ShapeDtypeStruct((s, d), BF16),
        compiler_params=_params("parallel"),
        name="rmsnorm",
    )(x, g.reshape(1, d))


def _gelu(x):
    return 0.5 * x * (1.0 + lax.erf(x * np.float32(np.sqrt(0.5))))


def _sigmoid(x):
    return 0.5 * (jnp.tanh(0.5 * x) + 1.0)


BF16_SUBLANES = 16


class _CastRider:
    def __init__(self, weights, gi, gj, n_cols=None):
        steps = gi * gj
        self.weights = list(weights)
        self.n_cols = n_cols
        self.rides = [w.shape[0] % (steps * BF16_SUBLANES) == 0 for w in self.weights]
        self.args = [w for w, r in zip(self.weights, self.rides) if r]
        width = lambda w: w.shape[1] if n_cols is None else n_cols
        self.in_specs = [pl.BlockSpec((w.shape[0] // steps, width(w)), lambda i, j: (i * gj + j, 0))
                         for w in self.args]
        self.out_specs = list(self.in_specs)
        self.out_shapes = [jax.ShapeDtypeStruct((w.shape[0], width(w)), BF16) for w in self.args]

    def body(self, in_refs, out_refs):
        for src, dst in zip(in_refs, out_refs):
            dst[...] = src[...].astype(dst.dtype)

    def results(self, outs):
        outs = list(outs)
        return [outs.pop(0) if r else w[:, :self.n_cols].astype(BF16)
                for w, r in zip(self.weights, self.rides)]


def _rider_specs(riders):
    cat = lambda name: [x for r in riders for x in getattr(r, name)]
    return cat("args"), cat("in_specs"), cat("out_specs"), cat("out_shapes")


def _split_refs(rest, riders):
    n_in = sum(len(r.in_specs) for r in riders)
    n_out = sum(len(r.out_specs) for r in riders)
    own = rest[:len(rest) - n_in - 1 - n_out]
    return own, rest[len(own) + n_in], rest[len(own):len(own) + n_in], rest[len(own) + n_in + 1:]


def _run_riders(riders, in_refs, out_refs):
    for r in riders:
        n_in, n_out = len(r.in_specs), len(r.out_specs)
        r.body(in_refs[:n_in], out_refs[:n_out])
        in_refs, out_refs = in_refs[n_in:], out_refs[n_out:]


def _rider_results(riders, outs):
    outs, res = list(outs), []
    for r in riders:
        n_out = len(r.out_specs)
        res.append(r.results(outs[:n_out]))
        outs = outs[n_out:]
    return res


def _proj_kernel(x_ref, w_ref, *rest, epilogue, riders):
    rest, o_ref, rider_in, rider_out = _split_refs(rest, riders)
    acc = jnp.dot(x_ref[...], w_ref[...], preferred_element_type=F32)
    if epilogue == "gelu":
        out = _gelu(acc)
    elif epilogue == "sigmoid":
        out = _sigmoid(acc)
    elif epilogue == "headnorm":
        gain_ref = rest[0]
        parts = []
        for h in range(acc.shape[1] // NA_HEAD_DIM):
            blk = acc[:, h * NA_HEAD_DIM:(h + 1) * NA_HEAD_DIM]
            ms = jnp.mean(blk * blk, axis=-1, keepdims=True)
            parts.append(blk * lax.rsqrt(ms + RMS_EPS))
        out = jnp.concatenate(parts, axis=1) * gain_ref[...]
    else:
        out = acc
    o_ref[...] = out.astype(o_ref.dtype)
    _run_riders(riders, rider_in, rider_out)


def _project(xn, w, col_start, n_cols, epilogue, gain=None, riders=()):
    s, d = xn.shape
    tm = _pick(s, (1024, 512, 256, 128))
    tn = _pick(int(np.gcd(n_cols, col_start)), (1024, 512, 256, 128))
    off = col_start // tn
    gi, gj = s // tm, n_cols // tn
    in_specs = [pl.BlockSpec((tm, d), lambda i, j: (i, 0)),
                pl.BlockSpec((d, tn), lambda i, j: (0, j + off))]
    args = [xn, w]
    if gain is not None:
        in_specs.append(pl.BlockSpec((1, tn), lambda i, j: (0, j)))
        args.append(gain)
    riders = tuple(make(gi, gj) for make in riders)
    r_args, r_in, r_out, r_shapes = _rider_specs(riders)
    outs = pl.pallas_call(
        functools.partial(_proj_kernel, epilogue=epilogue, riders=riders),
        grid=(gi, gj),
        in_specs=in_specs + r_in,
        out_specs=[pl.BlockSpec((tm, tn), lambda i, j: (i, j))] + r_out,
        out_shape=[jax.ShapeDtypeStruct((s, n_cols), BF16)] + r_shapes,
        compiler_params=_params("parallel", "parallel"),
        name="proj_" + epilogue,
    )(*args, *r_args)
    return (outs[0], *_rider_results(riders, outs[1:]))


def _first_proj_kernel(xn0_ref, w_ref, x_ref, g_ref, *rest, riders):
    n_in = sum(len(r.in_specs) for r in riders)
    rider_in, (o_ref, xn_ref), rider_out, x_scr = rest[:n_in], rest[n_in:n_in + 2], rest[n_in + 2:-2], rest[-2:]
    i, j = pl.program_id(0), pl.program_id(1)

    @pl.when((i == 0) & (j == 0))
    def _():
        x_scr[0][...] = xn0_ref[...]

    def step(slot):
        o_ref[...] = jnp.dot(x_scr[slot][...], w_ref[...], preferred_element_type=F32).astype(o_ref.dtype)
        rows = x_ref.shape[0]
        base = pl.multiple_of(j * rows, rows)
        x = x_ref[...]
        ms = jnp.mean(x * x, axis=-1, keepdims=True)
        xn = (x * lax.rsqrt(ms + RMS_EPS) * g_ref[...]).astype(xn_ref.dtype)
        xn_ref[...] = xn
        x_scr[1 - slot][pl.ds(base, rows), :] = xn
        _run_riders(riders, rider_in, rider_out)

    for slot in range(2):
        pl.when(i % 2 == slot)(functools.partial(step, slot))


def _first_project(x, g, w, riders=()):
    s, d = x.shape
    n_cols = w.shape[1]
    tm = _pick(s, (1024, 512, 256, 128))
    tn = _pick(n_cols, (512, 256, 128))
    gi, gj = s // tm, n_cols // tn
    rows = tm // gj
    assert rows * gj == tm and rows % BF16_SUBLANES == 0
    xn0 = _rmsnorm(x, g, rows=tm)
    riders = tuple(make(gi, gj) for make in riders)
    r_args, r_in, r_out, r_shapes = _rider_specs(riders)
    slab = lambda i, j: (((i + 1) % gi) * gj + j, 0)
    outs = pl.pallas_call(
        functools.partial(_first_proj_kernel, riders=riders),
        grid=(gi, gj),
        in_specs=[pl.BlockSpec((tm, d), lambda i, j: (0, 0), pipeline_mode=pl.Buffered(1)),
                  pl.BlockSpec((d, tn), lambda i, j: (0, j)),
                  pl.BlockSpec((rows, d), slab),
                  pl.BlockSpec((1, d), lambda i, j: (0, 0))] + r_in,
        out_specs=[pl.BlockSpec((tm, tn), lambda i, j: (i, j)),
                   pl.BlockSpec((rows, d), slab)] + r_out,
        out_shape=[jax.ShapeDtypeStruct((s, n_cols), BF16),
                   jax.ShapeDtypeStruct((s, d), BF16)] + r_shapes,
        scratch_shapes=[pltpu.VMEM((tm, d), BF16), pltpu.VMEM((tm, d), BF16)],
        compiler_params=_params("arbitrary", "arbitrary"),
        name="proj_first",
    )(xn0, w, x, g.reshape(1, d), *r_args)
    return (outs[0], outs[1], *_rider_results(riders, outs[2:]))


def _gmlp_kernel(u_ref, v_ref, lng_ref, lnb_ref, ws_ref, bs_ref, o_ref):
    v = v_ref[...].astype(F32)
    mu = jnp.mean(v, axis=-1, keepdims=True)
    xc = v - mu
    var = jnp.mean(xc * xc, axis=-1, keepdims=True)
    vn = (xc * lax.rsqrt(var + LN_EPS) * lng_ref[...] + lnb_ref[...]).astype(BF16)
    t, width = vn.shape
    fused = ws_ref.shape[2] // CHUNK
    span = fused * GM_GROUP_DIM
    zero = jnp.zeros((CHUNK, GM_GROUP_DIM), BF16)
    for c in range(t // CHUNK):
        rows = slice(c * CHUNK, (c + 1) * CHUNK)
        for p in range(width // span):
            cols = slice(p * span, (p + 1) * span)
            blocks = [vn[rows, p * span + q * GM_GROUP_DIM:p * span + (q + 1) * GM_GROUP_DIM]
                      for q in range(fused)]
            rhs = jnp.concatenate(
                [jnp.concatenate([blocks[q] if q2 == q else zero for q2 in range(fused)], axis=1)
                 for q in range(fused)], axis=0)
            mixed = jnp.dot(ws_ref[p], rhs, preferred_element_type=F32) + bs_ref[:, cols]
            o_ref[rows, cols] = (u_ref[rows, cols].astype(F32) * mixed).astype(o_ref.dtype)


GM_FUSED_GROUPS = 2


def _gmlp_operands(uv, ln_g, ln_b, w_s, b_s):
    width = uv.shape[1] // 2
    groups = w_s.shape[0]
    fused = GM_FUSED_GROUPS if groups % GM_FUSED_GROUPS == 0 else 1
    ws = jnp.concatenate([w_s[q::fused] for q in range(fused)], axis=2).astype(BF16)
    bs_full = jnp.repeat(b_s.T, GM_GROUP_DIM, axis=1)
    return [uv, uv, ln_g.reshape(1, width), ln_b.reshape(1, width), ws, bs_full]


def _gmlp_specs(t, operands, step):
    width = operands[0].shape[1] // 2
    const = lambda x: pl.BlockSpec(x.shape, lambda *idx: (0,) * x.ndim)
    in_specs = [pl.BlockSpec((t, width), lambda *idx: (step(*idx), 0)),
                pl.BlockSpec((t, width), lambda *idx: (step(*idx), 1))] + [const(x) for x in operands[2:]]
    return in_specs, pl.BlockSpec((t, width), lambda *idx: (step(*idx), 0))


def _gmlp(uv, ln_g, ln_b, w_s, b_s):
    s = uv.shape[0]
    width = uv.shape[1] // 2
    t = _pick(s, (512, 256, 128))
    operands = _gmlp_operands(uv, ln_g, ln_b, w_s, b_s)
    in_specs, out_spec = _gmlp_specs(t, operands, lambda i: i)
    return pl.pallas_call(
        _gmlp_kernel,
        grid=(s // t,),
        in_specs=in_specs,
        out_specs=out_spec,
        out_shape=jax.ShapeDtypeStruct((s, width), BF16),
        compiler_params=_params("parallel"),
        name="gmlp",
    )(*operands)


class _GmlpRider:
    def __init__(self, operands, gi, gj):
        uv = operands[0]
        s, width = uv.shape[0], uv.shape[1] // 2
        self.operands = operands
        t = s // (gi * gj)
        self.rides = t * gi * gj == s and t % CHUNK == 0
        self.args, self.in_specs, self.out_specs, self.out_shapes = [], [], [], []
        if self.rides:
            self.args = _gmlp_operands(*operands)
            self.in_specs, out_spec = _gmlp_specs(t, self.args, lambda i, j: i * gj + j)
            self.out_specs = [out_spec]
            self.out_shapes = [jax.ShapeDtypeStruct((s, width), BF16)]

    def body(self, in_refs, out_refs):
        if self.rides:
            _gmlp_kernel(*in_refs, *out_refs)

    def results(self, outs):
        return outs[0] if self.rides else _gmlp(*self.operands)


NA_MASKED_OFFSET = 2 * NA_WIN_H_MAX - 1
NA_PAIR = 2
NA_GROUP = 4


def _na_strip_plan(rows):
    kh = min(NA_WIN_H_MAX, rows)
    assert kh == NA_WIN_H_MAX and rows % NA_Q_ROWS == 0 and rows >= 2 * NA_K_ROWS
    assert NA_PAIR * GRID_W == 128 and NA_K_ROWS % NA_PAIR == 0
    starts = ((0, 0), (NA_Q_ROWS, 0), (rows - NA_Q_ROWS, rows - NA_K_ROWS))
    pairs, plan = [], []
    for r0, kb0 in starts:
        strips = []
        for rl in range(NA_Q_ROWS):
            r = r0 + rl
            rs = int(np.clip(r - kh // 2, 0, rows - kh))
            offs = [kb0 + krl - r + (NA_WIN_H_MAX - 1) if rs <= kb0 + krl < rs + kh else NA_MASKED_OFFSET
                    for krl in range(NA_K_ROWS)]
            row_pairs = [tuple(offs[j:j + NA_PAIR]) for j in range(0, NA_K_ROWS, NA_PAIR)]
            live = [j for j, pr in enumerate(row_pairs) if any(o != NA_MASKED_OFFSET for o in pr)]
            lo, hi = live[0], live[-1] + 1
            ids = []
            for pr in row_pairs[lo:hi]:
                if pr not in pairs:
                    pairs.append(pr)
                ids.append(pairs.index(pr))
            strips.append((lo, hi, tuple(ids)))
        plan.append(tuple(strips))
    return tuple(plan), np.asarray(pairs, np.int32)


def _na_offsets(qb, n_blocks):
    tq = NA_Q_ROWS * GRID_W
    kb = jnp.clip(qb - 1, 0, n_blocks - NA_K_ROWS // NA_Q_ROWS)
    return pl.multiple_of(qb * tq, tq), pl.multiple_of(kb * tq, tq)


def _na_scores(q_ref, k_ref, s_ref, qb, n_blocks):
    q0, k0 = _na_offsets(qb, n_blocks)
    q = q_ref[pl.ds(q0, NA_Q_ROWS * GRID_W), :]
    k = k_ref[pl.ds(k0, NA_K_ROWS * GRID_W), :]
    s_ref[...] = lax.dot_general(q, k, (((1,), (1,)), ((), ())), preferred_element_type=F32)


def _na_softmax_pv(s_ref, v_ref, pair_ref, o_ref, qb, n_blocks, strips):
    tq = NA_Q_ROWS * GRID_W
    tk = NA_K_ROWS * GRID_W
    lanes = NA_PAIR * GRID_W
    q0, k0 = _na_offsets(qb, n_blocks)
    v = v_ref[pl.ds(k0, tk), :]
    p_rows, l_rows = [], []
    for rl, (lo, hi, ids) in enumerate(strips):
        bias = jnp.concatenate([pair_ref[i] for i in ids], axis=1)
        s_rl = s_ref[rl * GRID_W:(rl + 1) * GRID_W, lo * lanes:hi * lanes] + bias
        m = jnp.max(s_rl, axis=-1, keepdims=True)
        p = jnp.exp(s_rl - m)
        l_rows.append(jnp.sum(p, axis=-1, keepdims=True))
        pieces = [p.astype(BF16)]
        if lo > 0:
            pieces.insert(0, jnp.zeros((GRID_W, lo * lanes), BF16))
        if hi * lanes < tk:
            pieces.append(jnp.zeros((GRID_W, tk - hi * lanes), BF16))
        p_rows.append(jnp.concatenate(pieces, axis=1))
    p = jnp.concatenate(p_rows, axis=0)
    o = jnp.dot(p, v, preferred_element_type=F32) / jnp.concatenate(l_rows, axis=0)
    o_ref[pl.ds(q0, tq), :] = o.astype(o_ref.dtype)


def _na_kernel(q_ref, k_ref, v_ref, pair_ref, o_ref, s_even, s_odd, *, n_blocks, group, plan):
    first, interior, last = plan
    n_groups = n_blocks // group
    bufs = (s_even, s_odd)

    def scores(g, buf):
        for u in range(group):
            _na_scores(q_ref, k_ref, buf.at[u], g * group + u, n_blocks)

    def finish(g, buf, strips):
        for u in range(group):
            _na_softmax_pv(buf.at[u], v_ref, pair_ref, o_ref, g * group + u, n_blocks, strips[u])

    def strips_of(g):
        strips = [interior] * group
        if g == 0:
            strips[0] = first
        if g == n_groups - 1:
            strips[-1] = last
        return strips

    def step(g, parity, strips):
        finish(g, bufs[parity], strips)
        scores(g + 1, bufs[1 - parity])

    scores(0, s_even)
    step(0, 0, strips_of(0))
    n_mid = n_groups - 2

    def body(t, carry):
        g = 1 + 2 * t
        step(g, 1, [interior] * group)
        step(g + 1, 0, [interior] * group)
        return carry

    lax.fori_loop(0, n_mid // 2, body, 0)
    if n_mid % 2:
        step(n_groups - 2, (n_groups - 2) % 2, strips_of(n_groups - 2))
    finish(n_groups - 1, bufs[(n_groups - 1) % 2], strips_of(n_groups - 1))


def _neighbourhood_attention(qk, v, rpb):
    s = v.shape[0]
    width = v.shape[1]
    heads = width // NA_HEAD_DIM
    rows = s // GRID_W
    n_blocks = rows // NA_Q_ROWS
    plan, pairs = _na_strip_plan(rows)
    c = np.arange(GRID_W)
    cs = np.clip(c - NA_WIN_W // 2, 0, GRID_W - NA_WIN_W)
    col_ok = (c[None, :] >= cs[:, None]) & (c[None, :] < cs[:, None] + NA_WIN_W)
    rel = c[None, :] - c[:, None] + (NA_WIN_W - 1)
    onehot = ((rel[None] == np.arange(2 * NA_WIN_W - 1)[:, None, None]) & col_ok[None]).astype(np.float32)
    tiles = jnp.einsum("hab,bck->hack", rpb.astype(F32), onehot, precision=lax.Precision.HIGHEST)
    tiles = jnp.where(col_ok[None, None], tiles, np.float32(MASK_VALUE))
    masked = jnp.full((heads, GRID_W, GRID_W), MASK_VALUE, F32)
    tile_of = lambda a: masked if a == NA_MASKED_OFFSET else tiles[:, a]
    pair_tiles = jnp.stack([jnp.concatenate([tile_of(int(a)) for a in pr], axis=-1) for pr in pairs],
                           axis=1)
    n_pairs = pairs.shape[0]
    group = _pick(n_blocks // 2, (NA_GROUP, 2, 1))
    s_buf = pltpu.VMEM((group, NA_Q_ROWS * GRID_W, NA_K_ROWS * GRID_W), F32)
    return pl.pallas_call(
        functools.partial(_na_kernel, n_blocks=n_blocks, group=group, plan=plan),
        grid=(heads,),
        scratch_shapes=[s_buf, s_buf],
        in_specs=[pl.BlockSpec((s, NA_HEAD_DIM), lambda h: (0, h)),
                  pl.BlockSpec((s, NA_HEAD_DIM), lambda h: (0, heads + h)),
                  pl.BlockSpec((s, NA_HEAD_DIM), lambda h: (0, h)),
                  pl.BlockSpec((None, n_pairs, GRID_W, NA_PAIR * GRID_W), lambda h: (h, 0, 0, 0))],
        out_specs=pl.BlockSpec((s, NA_HEAD_DIM), lambda h: (0, h)),
        out_shape=jax.ShapeDtypeStruct((s, width), BF16),
        compiler_params=_params("parallel"),
        name="natten",
    )(qk, qk, v, pair_tiles)


def _merge_kernel(ya_ref, yb_ref, wa_ref, wb_ref, ga_ref, gb_ref, o_ref):
    a = jnp.dot(ya_ref[...], wa_ref[...], preferred_element_type=F32)
    b = jnp.dot(yb_ref[...], wb_ref[...], preferred_element_type=F32)
    o_ref[...] = (ga_ref[...].astype(F32) * a + gb_ref[...].astype(F32) * b).astype(o_ref.dtype)


def _merge(y_a, y_b, w_oa, w_ob, sg):
    s, k = y_a.shape
    n = w_oa.shape[1]
    tm = _pick(s, (1024, 512, 256, 128))
    tn = _pick(n, (1024, 512, 256, 128))
    gb_off = n // tn
    return pl.pallas_call(
        _merge_kernel,
        grid=(s // tm, n // tn),
        in_specs=[pl.BlockSpec((tm, k), lambda i, j: (i, 0)),
                  pl.BlockSpec((tm, k), lambda i, j: (i, 0)),
                  pl.BlockSpec((k, tn), lambda i, j: (0, j)),
                  pl.BlockSpec((k, tn), lambda i, j: (0, j)),
                  pl.BlockSpec((tm, tn), lambda i, j: (i, j)),
                  pl.BlockSpec((tm, tn), lambda i, j: (i, j + gb_off))],
        out_specs=pl.BlockSpec((tm, tn), lambda i, j: (i, j)),
        out_shape=jax.ShapeDtypeStruct((s, n), BF16),
        compiler_params=_params("parallel", "parallel"),
        name="merge",
    )(y_a, y_b, w_oa, w_ob, sg, sg)


def _matmul_residual_kernel(x_ref, w_ref, r_ref, o_ref):
    o_ref[...] = r_ref[...] + jnp.dot(x_ref[...], w_ref[...], preferred_element_type=F32)


def _matmul_residual(lhs, w, residual, tm_prefs, tn_prefs, name):
    s, k = lhs.shape
    n = w.shape[1]
    tm = _pick(s, tm_prefs)
    tn = _pick(n, tn_prefs)
    return pl.pallas_call(
        _matmul_residual_kernel,
        grid=(s // tm, n // tn),
        in_specs=[pl.BlockSpec((tm, k), lambda i, j: (i, 0)),
                  pl.BlockSpec((k, tn), lambda i, j: (0, j)),
                  pl.BlockSpec((tm, tn), lambda i, j: (i, j))],
        out_specs=pl.BlockSpec((tm, tn), lambda i, j: (i, j)),
        out_shape=jax.ShapeDtypeStruct((s, n), F32),
        compiler_params=_params("parallel", "parallel"),
        name=name,
    )(lhs, w, residual)


def _matmul_residual_wide_k(lhs, w, residual, name):
    s, k = lhs.shape
    n = w.shape[1]
    tm = _pick(s, (512, 256, 128))
    tn = _pick(n, (1024, 512, 256, 128))
    return pl.pallas_call(
        _matmul_residual_kernel,
        grid=(n // tn, s // tm),
        in_specs=[pl.BlockSpec((tm, k), lambda j, i: (i, 0)),
                  pl.BlockSpec((k, tn), lambda j, i: (0, j), pipeline_mode=pl.Buffered(1)),
                  pl.BlockSpec((tm, tn), lambda j, i: (i, j))],
        out_specs=pl.BlockSpec((tm, tn), lambda j, i: (i, j)),
        out_shape=jax.ShapeDtypeStruct((s, n), F32),
        compiler_params=_params("parallel", "parallel"),
        name=name,
    )(lhs, w, residual)


def _swiglu_kernel(x_ref, wg_ref, wu_ref, *rest, riders):
    _, o_ref, rider_in, rider_out = _split_refs(rest, riders)
    x = x_ref[...]
    g = jnp.dot(x, wg_ref[...], preferred_element_type=F32)
    u = jnp.dot(x, wu_ref[...], preferred_element_type=F32)
    o_ref[...] = (g * _sigmoid(g) * u).astype(o_ref.dtype)
    _run_riders(riders, rider_in, rider_out)


def _swiglu_up(hn, w_gate, w_up, riders=()):
    s, d = hn.shape
    f = w_gate.shape[1]
    tm = _pick(s, (4096, 2048, 1024, 512, 256, 128))
    tn = _pick(f, (256, 128))
    gi, gj = s // tm, f // tn
    riders = tuple(make(gi, gj) for make in riders)
    r_args, r_in, r_out, r_shapes = _rider_specs(riders)
    outs = pl.pallas_call(
        functools.partial(_swiglu_kernel, riders=riders),
        grid=(gi, gj),
        in_specs=[pl.BlockSpec((tm, d), lambda i, j: (i, 0), pipeline_mode=pl.Buffered(1)),
                  pl.BlockSpec((d, tn), lambda i, j: (0, j)),
                  pl.BlockSpec((d, tn), lambda i, j: (0, j))] + r_in,
        out_specs=[pl.BlockSpec((tm, tn), lambda i, j: (i, j))] + r_out,
        out_shape=[jax.ShapeDtypeStruct((s, f), BF16)] + r_shapes,
        compiler_params=_params("parallel", "parallel"),
        name="swiglu_up",
    )(hn, w_gate, w_up, *r_args)
    return (outs[0], *_rider_results(riders, outs[1:]))


def kernel(x, norm1_g, w_in, gm_ln_g, gm_ln_b, gm_w_s, gm_b_s, q_gain, k_gain, na_rpb,
           w_o_gm, w_o_na, w_out, norm2_g, w_ff_gate, w_ff_up, w_ff_down):
    batch, seq, d_model = x.shape
    depth = w_in.shape[0]
    gm_width = gm_ln_g.shape[1]
    na_width = w_o_na.shape[1]
    heads = na_width // NA_HEAD_DIM
    outs = []
    for bi in range(batch):
        h = x.reshape(seq, d_model) if batch == 1 else x[bi]
        for layer in range(depth):
            cast = lambda *ws: functools.partial(_CastRider, ws)
            v_col = 2 * gm_width + 2 * na_width
            w_v = w_in[layer][:, v_col:v_col + na_width].astype(BF16)
            v, xn, (w_uv,) = _first_project(
                h, norm1_g[layer], w_v,
                riders=(functools.partial(_CastRider, (w_in[layer],), n_cols=2 * gm_width),))
            uv, (w_in_l,) = _project(xn, w_uv, 0, 2 * gm_width, "gelu", riders=(cast(w_in[layer]),))
            gains = jnp.concatenate([jnp.tile(q_gain[layer] * np.float32(NA_HEAD_DIM ** -0.5), heads),
                                     jnp.tile(k_gain[layer], heads)]).reshape(1, 2 * na_width)
            qk, (w_oa, w_ob, w_o) = _project(xn, w_in_l, 2 * gm_width, 2 * na_width, "headnorm", gains,
                                             riders=(cast(w_o_gm[layer], w_o_na[layer], w_out[layer]),))
            gmlp = functools.partial(
                _GmlpRider, (uv, gm_ln_g[layer], gm_ln_b[layer], gm_w_s[layer], gm_b_s[layer]))
            sg, (w_gate, w_up), y_a = _project(
                xn, w_in_l, 2 * gm_width + 3 * na_width, 2 * d_model, "sigmoid",
                riders=(cast(w_ff_gate[layer], w_ff_up[layer]), gmlp))
            y_b = _neighbourhood_attention(qk, v, na_rpb[layer])
            merged = _merge(y_a, y_b, w_oa, w_ob, sg)
            h = _matmul_residual(merged, w_o, h,
                                 (1024, 512, 256, 128), (1024, 512, 256, 128), "out_proj")
            hn = _rmsnorm(h, norm2_g[layer])
            act, (w_down,) = _swiglu_up(hn, w_gate, w_up, riders=(cast(w_ff_down[layer]),))
            h = _matmul_residual_wide_k(act, w_down, h, "ffn_down")
        outs.append(h)
    if batch == 1:
        return outs[0].reshape(1, seq, d_model)
    return jnp.stack(outs)
```

```python
import functools

import numpy as np
import jax
import jax.numpy as jnp
from jax import lax
from jax.experimental import pallas as pl
from jax.experimental.pallas import tpu as pltpu

GRID_W = 64
CHUNK = 128
GM_GROUP_DIM = 128
NA_HEAD_DIM = 128
NA_WIN_H_MAX = 8
NA_WIN_W = 16
RMS_EPS = 1e-6
LN_EPS = 1e-5

V7X_VMEM_BYTES = 64 * 1024 * 1024
VMEM_LIMIT_BYTES = V7X_VMEM_BYTES - 6 * 1024 * 1024

NA_Q_ROWS = 4
NA_K_ROWS = NA_Q_ROWS + NA_WIN_H_MAX
MASK_VALUE = -1e30

F32 = jnp.float32
BF16 = jnp.bfloat16


def _params(*sem):
    return pltpu.CompilerParams(dimension_semantics=sem, vmem_limit_bytes=VMEM_LIMIT_BYTES)


def _pick(n, prefs):
    for p in prefs:
        if n % p == 0:
            return p
    return n


def _rmsnorm_kernel(x_ref, g_ref, o_ref):
    x = x_ref[...]
    ms = jnp.mean(x * x, axis=-1, keepdims=True)
    o_ref[...] = (x * lax.rsqrt(ms + RMS_EPS) * g_ref[...]).astype(o_ref.dtype)


def _rmsnorm(x, g, rows=None):
    d = x.shape[1]
    s = x.shape[0] if rows is None else rows
    tr = _pick(s, (512, 256, 128))
    return pl.pallas_call(
        _rmsnorm_kernel,
        grid=(s // tr,),
        in_specs=[pl.BlockSpec((tr, d), lambda i: (i, 0)),
                  pl.BlockSpec((1, d), lambda i: (0, 0))],
        out_specs=pl.BlockSpec((tr, d), lambda i: (i, 0)),
        out_shape=jax.ShapeDtypeStruct((s, d), BF16),
        compiler_params=_params("parallel"),
        name="rmsnorm",
    )(x, g.reshape(1, d))


def _gelu(x):
    return 0.5 * x * (1.0 + lax.erf(x * np.float32(np.sqrt(0.5))))


def _sigmoid(x):
    return 0.5 * (jnp.tanh(0.5 * x) + 1.0)


BF16_SUBLANES = 16


class _CastRider:
    def __init__(self, weights, gi, gj, n_cols=None):
        steps = gi * gj
        self.weights = list(weights)
        self.n_cols = n_cols
        self.rides = [w.shape[0] % (steps * BF16_SUBLANES) == 0 for w in self.weights]
        self.args = [w for w, r in zip(self.weights, self.rides) if r]
        width = lambda w: w.shape[1] if n_cols is None else n_cols
        self.in_specs = [pl.BlockSpec((w.shape[0] // steps, width(w)), lambda i, j: (i * gj + j, 0))
                         for w in self.args]
        self.out_specs = list(self.in_specs)
        self.out_shapes = [jax.ShapeDtypeStruct((w.shape[0], width(w)), BF16) for w in self.args]

    def body(self, in_refs, out_refs):
        for src, dst in zip(in_refs, out_refs):
            dst[...] = src[...].astype(dst.dtype)

    def results(self, outs):
        outs = list(outs)
        return [outs.pop(0) if r else w[:, :self.n_cols].astype(BF16)
                for w, r in zip(self.weights, self.rides)]


def _rider_specs(riders):
    cat = lambda name: [x for r in riders for x in getattr(r, name)]
    return cat("args"), cat("in_specs"), cat("out_specs"), cat("out_shapes")


def _split_refs(rest, riders):
    n_in = sum(len(r.in_specs) for r in riders)
    n_out = sum(len(r.out_specs) for r in riders)
    own = rest[:len(rest) - n_in - 1 - n_out]
    return own, rest[len(own) + n_in], rest[len(own):len(own) + n_in], rest[len(own) + n_in + 1:]


def _run_riders(riders, in_refs, out_refs):
    for r in riders:
        n_in, n_out = len(r.in_specs), len(r.out_specs)
        r.body(in_refs[:n_in], out_refs[:n_out])
        in_refs, out_refs = in_refs[n_in:], out_refs[n_out:]


def _rider_results(riders, outs):
    outs, res = list(outs), []
    for r in riders:
        n_out = len(r.out_specs)
        res.append(r.results(outs[:n_out]))
        outs = outs[n_out:]
    return res


def _proj_kernel(x_ref, w_ref, *rest, epilogue, riders):
    rest, o_ref, rider_in, rider_out = _split_refs(rest, riders)
    acc = jnp.dot(x_ref[...], w_ref[...], preferred_element_type=F32)
    if epilogue == "gelu":
        out = _gelu(acc)
    elif epilogue == "sigmoid":
        out = _sigmoid(acc)
    elif epilogue == "headnorm":
        gain_ref = rest[0]
        parts = []
        for h in range(acc.shape[1] // NA_HEAD_DIM):
            blk = acc[:, h * NA_HEAD_DIM:(h + 1) * NA_HEAD_DIM]
            ms = jnp.mean(blk * blk, axis=-1, keepdims=True)
            parts.append(blk * lax.rsqrt(ms + RMS_EPS))
        out = jnp.concatenate(parts, axis=1) * gain_ref[...]
    else:
        out = acc
    o_ref[...] = out.astype(o_ref.dtype)
    _run_riders(riders, rider_in, rider_out)


def _project(xn, w, col_start, n_cols, epilogue, gain=None, riders=()):
    s, d = xn.shape
    tm = _pick(s, (1024, 512, 256, 128))
    tn = _pick(int(np.gcd(n_cols, col_start)), (1024, 512, 256, 128))
    off = col_start // tn
    gi, gj = s // tm, n_cols // tn
    in_specs = [pl.BlockSpec((tm, d), lambda i, j: (i, 0)),
                pl.BlockSpec((d, tn), lambda i, j: (0, j + off))]
    args = [xn, w]
    if gain is not None:
        in_specs.append(pl.BlockSpec((1, tn), lambda i, j: (0, j)))
        args.append(gain)
    riders = tuple(make(gi, gj) for make in riders)
    r_args, r_in, r_out, r_shapes = _rider_specs(riders)
    outs = pl.pallas_call(
        functools.partial(_proj_kernel, epilogue=epilogue, riders=riders),
        grid=(gi, gj),
        in_specs=in_specs + r_in,
        out_specs=[pl.BlockSpec((tm, tn), lambda i, j: (i, j))] + r_out,
        out_shape=[jax.ShapeDtypeStruct((s, n_cols), BF16)] + r_shapes,
        compiler_params=_params("parallel", "parallel"),
        name="proj_" + epilogue,
    )(*args, *r_args)
    return (outs[0], *_rider_results(riders, outs[1:]))


def _first_proj_kernel(xn0_ref, w_ref, x_ref, g_ref, *rest, riders):
    n_in = sum(len(r.in_specs) for r in riders)
    rider_in, (o_ref, xn_ref), rider_out, x_scr = rest[:n_in], rest[n_in:n_in + 2], rest[n_in + 2:-2], rest[-2:]
    i, j = pl.program_id(0), pl.program_id(1)

    @pl.when((i == 0) & (j == 0))
    def _():
        x_scr[0][...] = xn0_ref[...]

    def step(slot):
        o_ref[...] = jnp.dot(x_scr[slot][...], w_ref[...], preferred_element_type=F32).astype(o_ref.dtype)
        rows = x_ref.shape[0]
        base = pl.multiple_of(j * rows, rows)
        x = x_ref[...]
        ms = jnp.mean(x * x, axis=-1, keepdims=True)
        xn = (x * lax.rsqrt(ms + RMS_EPS) * g_ref[...]).astype(xn_ref.dtype)
        xn_ref[...] = xn
        x_scr[1 - slot][pl.ds(base, rows), :] = xn
        _run_riders(riders, rider_in, rider_out)

    for slot in range(2):
        pl.when(i % 2 == slot)(functools.partial(step, slot))


def _first_project(x, g, w, riders=()):
    s, d = x.shape
    n_cols = w.shape[1]
    tm = _pick(s, (1024, 512, 256, 128))
    tn = _pick(n_cols, (512, 256, 128))
    gi, gj = s // tm, n_cols // tn
    rows = tm // gj
    assert rows * gj == tm and rows % BF16_SUBLANES == 0
    xn0 = _rmsnorm(x, g, rows=tm)
    riders = tuple(make(gi, gj) for make in riders)
    r_args, r_in, r_out, r_shapes = _rider_specs(riders)
    slab = lambda i, j: (((i + 1) % gi) * gj + j, 0)
    outs = pl.pallas_call(
        functools.partial(_first_proj_kernel, riders=riders),
        grid=(gi, gj),
        in_specs=[pl.BlockSpec((tm, d), lambda i, j: (0, 0), pipeline_mode=pl.Buffered(1)),
                  pl.BlockSpec((d, tn), lambda i, j: (0, j)),
                  pl.BlockSpec((rows, d), slab),
                  pl.BlockSpec((1, d), lambda i, j: (0, 0))] + r_in,
        out_specs=[pl.BlockSpec((tm, tn), lambda i, j: (i, j)),
                   pl.BlockSpec((rows, d), slab)] + r_out,
        out_shape=[jax.ShapeDtypeStruct((s, n_cols), BF16),
                   jax.ShapeDtypeStruct((s, d), BF16)] + r_shapes,
        scratch_shapes=[pltpu.VMEM((tm, d), BF16), pltpu.VMEM((tm, d), BF16)],
        compiler_params=_params("arbitrary", "arbitrary"),
        name="proj_first",
    )(xn0, w, x, g.reshape(1, d), *r_args)
    return (outs[0], outs[1], *_rider_results(riders, outs[2:]))


def _gmlp_kernel(u_ref, v_ref, lng_ref, lnb_ref, ws_ref, bs_ref, o_ref):
    v = v_ref[...].astype(F32)
    mu = jnp.mean(v, axis=-1, keepdims=True)
    xc = v - mu
    var = jnp.mean(xc * xc, axis=-1, keepdims=True)
    vn = (xc * lax.rsqrt(var + LN_EPS) * lng_ref[...] + lnb_ref[...]).astype(BF16)
    t, width = vn.shape
    fused = ws_ref.shape[2] // CHUNK
    span = fused * GM_GROUP_DIM
    zero = jnp.zeros((CHUNK, GM_GROUP_DIM), BF16)
    for c in range(t // CHUNK):
        rows = slice(c * CHUNK, (c + 1) * CHUNK)
        for p in range(width // span):
            cols = slice(p * span, (p + 1) * span)
            blocks = [vn[rows, p * span + q * GM_GROUP_DIM:p * span + (q + 1) * GM_GROUP_DIM]
                      for q in range(fused)]
            rhs = jnp.concatenate(
                [jnp.concatenate([blocks[q] if q2 == q else zero for q2 in range(fused)], axis=1)
                 for q in range(fused)], axis=0)
            mixed = jnp.dot(ws_ref[p], rhs, preferred_element_type=F32) + bs_ref[:, cols]
            o_ref[rows, cols] = (u_ref[rows, cols].astype(F32) * mixed).astype(o_ref.dtype)


GM_FUSED_GROUPS = 2


def _gmlp_operands(uv, ln_g, ln_b, w_s, b_s):
    width = uv.shape[1] // 2
    groups = w_s.shape[0]
    fused = GM_FUSED_GROUPS if groups % GM_FUSED_GROUPS == 0 else 1
    ws = jnp.concatenate([w_s[q::fused] for q in range(fused)], axis=2).astype(BF16)
    bs_full = jnp.repeat(b_s.T, GM_GROUP_DIM, axis=1)
    return [uv, uv, ln_g.reshape(1, width), ln_b.reshape(1, width), ws, bs_full]


def _gmlp_specs(t, operands, step):
    width = operands[0].shape[1] // 2
    const = lambda x: pl.BlockSpec(x.shape, lambda *idx: (0,) * x.ndim)
    in_specs = [pl.BlockSpec((t, width), lambda *idx: (step(*idx), 0)),
                pl.BlockSpec((t, width), lambda *idx: (step(*idx), 1))] + [const(x) for x in operands[2:]]
    return in_specs, pl.BlockSpec((t, width), lambda *idx: (step(*idx), 0))


def _gmlp(uv, ln_g, ln_b, w_s, b_s):
    s = uv.shape[0]
    width = uv.shape[1] // 2
    t = _pick(s, (512, 256, 128))
    operands = _gmlp_operands(uv, ln_g, ln_b, w_s, b_s)
    in_specs, out_spec = _gmlp_specs(t, operands, lambda i: i)
    return pl.pallas_call(
        _gmlp_kernel,
        grid=(s // t,),
        in_specs=in_specs,
        out_specs=out_spec,
        out_shape=jax.ShapeDtypeStruct((s, width), BF16),
        compiler_params=_params("parallel"),
        name="gmlp",
    )(*operands)


class _GmlpRider:
    def __init__(self, operands, gi, gj):
        uv = operands[0]
        s, width = uv.shape[0], uv.shape[1] // 2
        self.operands = operands
        t = s // (gi * gj)
        self.rides = t * gi * gj == s and t % CHUNK == 0
        self.args, self.in_specs, self.out_specs, self.out_shapes = [], [], [], []
        if self.rides:
            self.args = _gmlp_operands(*operands)
            self.in_specs, out_spec = _gmlp_specs(t, self.args, lambda i, j: i * gj + j)
            self.out_specs = [out_spec]
            self.out_shapes = [jax.ShapeDtypeStruct((s, width), BF16)]

    def body(self, in_refs, out_refs):
        if self.rides:
            _gmlp_kernel(*in_refs, *out_refs)

    def results(self, outs):
        return outs[0] if self.rides else _gmlp(*self.operands)


NA_MASKED_OFFSET = 2 * NA_WIN_H_MAX - 1
NA_PAIR = 2
NA_GROUP = 4


def _na_strip_plan(rows):
    kh = min(NA_WIN_H_MAX, rows)
    assert kh == NA_WIN_H_MAX and rows % NA_Q_ROWS == 0 and rows >= 2 * NA_K_ROWS
    assert NA_PAIR * GRID_W == 128 and NA_K_ROWS % NA_PAIR == 0
    starts = ((0, 0), (NA_Q_ROWS, 0), (rows - NA_Q_ROWS, rows - NA_K_ROWS))
    pairs, plan = [], []
    for r0, kb0 in starts:
        strips = []
        for rl in range(NA_Q_ROWS):
            r = r0 + rl
            rs = int(np.clip(r - kh // 2, 0, rows - kh))
            offs = [kb0 + krl - r + (NA_WIN_H_MAX - 1) if rs <= kb0 + krl < rs + kh else NA_MASKED_OFFSET
                    for krl in range(NA_K_ROWS)]
            row_pairs = [tuple(offs[j:j + NA_PAIR]) for j in range(0, NA_K_ROWS, NA_PAIR)]
            live = [j for j, pr in enumerate(row_pairs) if any(o != NA_MASKED_OFFSET for o in pr)]
            lo, hi = live[0], live[-1] + 1
            ids = []
            for pr in row_pairs[lo:hi]:
                if pr not in pairs:
                    pairs.append(pr)
                ids.append(pairs.index(pr))
            strips.append((lo, hi, tuple(ids)))
        plan.append(tuple(strips))
    return tuple(plan), np.asarray(pairs, np.int32)


def _na_offsets(qb, n_blocks):
    tq = NA_Q_ROWS * GRID_W
    kb = jnp.clip(qb - 1, 0, n_blocks - NA_K_ROWS // NA_Q_ROWS)
    return pl.multiple_of(qb * tq, tq), pl.multiple_of(kb * tq, tq)


def _na_scores(q_ref, k_ref, s_ref, qb, n_blocks):
    q0, k0 = _na_offsets(qb, n_blocks)
    q = q_ref[pl.ds(q0, NA_Q_ROWS * GRID_W), :]
    k = k_ref[pl.ds(k0, NA_K_ROWS * GRID_W), :]
    s_ref[...] = lax.dot_general(q, k, (((1,), (1,)), ((), ())), preferred_element_type=F32)


def _na_softmax_pv(s_ref, v_ref, pair_ref, o_ref, qb, n_blocks, strips):
    tq = NA_Q_ROWS * GRID_W
    tk = NA_K_ROWS * GRID_W
    lanes = NA_PAIR * GRID_W
    q0, k0 = _na_offsets(qb, n_blocks)
    v = v_ref[pl.ds(k0, tk), :]
    p_rows, l_rows = [], []
    for rl, (lo, hi, ids) in enumerate(strips):
        bias = jnp.concatenate([pair_ref[i] for i in ids], axis=1)
        s_rl = s_ref[rl * GRID_W:(rl + 1) * GRID_W, lo * lanes:hi * lanes] + bias
        m = jnp.max(s_rl, axis=-1, keepdims=True)
        p = jnp.exp(s_rl - m)
        l_rows.append(jnp.sum(p, axis=-1, keepdims=True))
        pieces = [p.astype(BF16)]
        if lo > 0:
            pieces.insert(0, jnp.zeros((GRID_W, lo * lanes), BF16))
        if hi * lanes < tk:
            pieces.append(jnp.zeros((GRID_W, tk - hi * lanes), BF16))
        p_rows.append(jnp.concatenate(pieces, axis=1))
    p = jnp.concatenate(p_rows, axis=0)
    o = jnp.dot(p, v, preferred_element_type=F32) / jnp.concatenate(l_rows, axis=0)
    o_ref[pl.ds(q0, tq), :] = o.astype(o_ref.dtype)


def _na_kernel(q_ref, k_ref, v_ref, pair_ref, o_ref, s_even, s_odd, *, n_blocks, group, plan):
    first, interior, last = plan
    n_groups = n_blocks // group
    bufs = (s_even, s_odd)

    def scores(g, buf):
        for u in range(group):
            _na_scores(q_ref, k_ref, buf.at[u], g * group + u, n_blocks)

    def finish(g, buf, strips):
        for u in range(group):
            _na_softmax_pv(buf.at[u], v_ref, pair_ref, o_ref, g * group + u, n_blocks, strips[u])

    def strips_of(g):
        strips = [interior] * group
        if g == 0:
            strips[0] = first
        if g == n_groups - 1:
            strips[-1] = last
        return strips

    def step(g, parity, strips):
        finish(g, bufs[parity], strips)
        scores(g + 1, bufs[1 - parity])

    scores(0, s_even)
    step(0, 0, strips_of(0))
    n_mid = n_groups - 2

    def body(t, carry):
        g = 1 + 2 * t
        step(g, 1, [interior] * group)
        step(g + 1, 0, [interior] * group)
        return carry

    lax.fori_loop(0, n_mid // 2, body, 0)
    if n_mid % 2:
        step(n_groups - 2, (n_groups - 2) % 2, strips_of(n_groups - 2))
    finish(n_groups - 1, bufs[(n_groups - 1) % 2], strips_of(n_groups - 1))


def _neighbourhood_attention(qk, v, rpb):
    s = v.shape[0]
    width = v.shape[1]
    heads = width // NA_HEAD_DIM
    rows = s // GRID_W
    n_blocks = rows // NA_Q_ROWS
    plan, pairs = _na_strip_plan(rows)
    c = np.arange(GRID_W)
    cs = np.clip(c - NA_WIN_W // 2, 0, GRID_W - NA_WIN_W)
    col_ok = (c[None, :] >= cs[:, None]) & (c[None, :] < cs[:, None] + NA_WIN_W)
    rel = c[None, :] - c[:, None] + (NA_WIN_W - 1)
    onehot = ((rel[None] == np.arange(2 * NA_WIN_W - 1)[:, None, None]) & col_ok[None]).astype(np.float32)
    tiles = jnp.einsum("hab,bck->hack", rpb.astype(F32), onehot, precision=lax.Precision.HIGHEST)
    tiles = jnp.where(col_ok[None, None], tiles, np.float32(MASK_VALUE))
    masked = jnp.full((heads, GRID_W, GRID_W), MASK_VALUE, F32)
    tile_of = lambda a: masked if a == NA_MASKED_OFFSET else tiles[:, a]
    pair_tiles = jnp.stack([jnp.concatenate([tile_of(int(a)) for a in pr], axis=-1) for pr in pairs],
                           axis=1)
    n_pairs = pairs.shape[0]
    group = _pick(n_blocks // 2, (NA_GROUP, 2, 1))
    s_buf = pltpu.VMEM((group, NA_Q_ROWS * GRID_W, NA_K_ROWS * GRID_W), F32)
    return pl.pallas_call(
        functools.partial(_na_kernel, n_blocks=n_blocks, group=group, plan=plan),
        grid=(heads,),
        scratch_shapes=[s_buf, s_buf],
        in_specs=[pl.BlockSpec((s, NA_HEAD_DIM), lambda h: (0, h)),
                  pl.BlockSpec((s, NA_HEAD_DIM), lambda h: (0, heads + h)),
                  pl.BlockSpec((s, NA_HEAD_DIM), lambda h: (0, h)),
                  pl.BlockSpec((None, n_pairs, GRID_W, NA_PAIR * GRID_W), lambda h: (h, 0, 0, 0))],
        out_specs=pl.BlockSpec((s, NA_HEAD_DIM), lambda h: (0, h)),
        out_shape=jax.ShapeDtypeStruct((s, width), BF16),
        compiler_params=_params("parallel"),
        name="natten",
    )(qk, qk, v, pair_tiles)


def _merge_kernel(ya_ref, yb_ref, wa_ref, wb_ref, ga_ref, gb_ref, o_ref):
    a = jnp.dot(ya_ref[...], wa_ref[...], preferred_element_type=F32)
    b = jnp.dot(yb_ref[...], wb_ref[...], preferred_element_type=F32)
    o_ref[...] = (ga_ref[...].astype(F32) * a + gb_ref[...].astype(F32) * b).astype(o_ref.dtype)


def _merge(y_a, y_b, w_oa, w_ob, sg):
    s, k = y_a.shape
    n = w_oa.shape[1]
    tm = _pick(s, (1024, 512, 256, 128))
    tn = _pick(n, (1024, 512, 256, 128))
    gb_off = n // tn
    return pl.pallas_call(
        _merge_kernel,
        grid=(s // tm, n // tn),
        in_specs=[pl.BlockSpec((tm, k), lambda i, j: (i, 0)),
                  pl.BlockSpec((tm, k), lambda i, j: (i, 0)),
                  pl.BlockSpec((k, tn), lambda i, j: (0, j)),
                  pl.BlockSpec((k, tn), lambda i, j: (0, j)),
                  pl.BlockSpec((tm, tn), lambda i, j: (i, j)),
                  pl.BlockSpec((tm, tn), lambda i, j: (i, j + gb_off))],
        out_specs=pl.BlockSpec((tm, tn), lambda i, j: (i, j)),
        out_shape=jax.ShapeDtypeStruct((s, n), BF16),
        compiler_params=_params("parallel", "parallel"),
        name="merge",
    )(y_a, y_b, w_oa, w_ob, sg, sg)


def _matmul_residual_kernel(x_ref, w_ref, r_ref, o_ref):
    o_ref[...] = r_ref[...] + jnp.dot(x_ref[...], w_ref[...], preferred_element_type=F32)


def _matmul_residual(lhs, w, residual, tm_prefs, tn_prefs, name):
    s, k = lhs.shape
    n = w.shape[1]
    tm = _pick(s, tm_prefs)
    tn = _pick(n, tn_prefs)
    return pl.pallas_call(
        _matmul_residual_kernel,
        grid=(s // tm, n // tn),
        in_specs=[pl.BlockSpec((tm, k), lambda i, j: (i, 0)),
                  pl.BlockSpec((k, tn), lambda i, j: (0, j)),
                  pl.BlockSpec((tm, tn), lambda i, j: (i, j))],
        out_specs=pl.BlockSpec((tm, tn), lambda i, j: (i, j)),
        out_shape=jax.ShapeDtypeStruct((s, n), F32),
        compiler_params=_params("parallel", "parallel"),
        name=name,
    )(lhs, w, residual)


def _matmul_residual_wide_k(lhs, w, residual, name):
    s, k = lhs.shape
    n = w.shape[1]
    tm = _pick(s, (512, 256, 128))
    tn = _pick(n, (1024, 512, 256, 128))
    return pl.pallas_call(
        _matmul_residual_kernel,
        grid=(n // tn, s // tm),
        in_specs=[pl.BlockSpec((tm, k), lambda j, i: (i, 0)),
                  pl.BlockSpec((k, tn), lambda j, i: (0, j), pipeline_mode=pl.Buffered(1)),
                  pl.BlockSpec((tm, tn), lambda j, i: (i, j))],
        out_specs=pl.BlockSpec((tm, tn), lambda j, i: (i, j)),
        out_shape=jax.ShapeDtypeStruct((s, n), F32),
        compiler_params=_params("parallel", "parallel"),
        name=name,
    )(lhs, w, residual)


def _swiglu_kernel(hn0_hbm, wg_ref, wu_ref, h_ref, gain_ref, *rest, riders, n_slabs):
    n_in = sum(len(r.in_specs) for r in riders)
    rider_in, o_ref, rider_out, x_scr = rest[:n_in], rest[n_in], rest[n_in + 1:-2], rest[-2:]
    i, j = pl.program_id(0), pl.program_id(1)

    @pl.when((i == 0) & (j == 0))
    def _():
        pltpu.sync_copy(hn0_hbm, x_scr[0])

    def step(slot):
        x = x_scr[slot][...]
        g = jnp.dot(x, wg_ref[...], preferred_element_type=F32)
        u = jnp.dot(x, wu_ref[...], preferred_element_type=F32)
        o_ref[...] = (g * _sigmoid(g) * u).astype(o_ref.dtype)
        h = h_ref[...]
        ms = jnp.mean(h * h, axis=-1, keepdims=True)
        hn = (h * lax.rsqrt(ms + RMS_EPS) * gain_ref[...]).astype(BF16)
        rows = hn.shape[0]
        base = pl.multiple_of(jnp.minimum(j, n_slabs - 1) * rows, rows)
        x_scr[1 - slot][pl.ds(base, rows), :] = hn
        _run_riders(riders, rider_in, rider_out)

    for slot in range(2):
        pl.when(i % 2 == slot)(functools.partial(step, slot))


def _swiglu_up(h, gain, w_gate, w_up, riders=()):
    s, d = h.shape
    f = w_gate.shape[1]
    tm = _pick(s, (2048, 1024, 512, 256, 128))
    tn = _pick(f, (256, 128))
    gi, gj = s // tm, f // tn
    n_slabs = 1
    while n_slabs * 2 <= gj and tm % (n_slabs * 2 * BF16_SUBLANES) == 0:
        n_slabs *= 2
    rows = tm // n_slabs
    hn0 = _rmsnorm(h, gain, rows=tm)
    riders = tuple(make(gi, gj) for make in riders)
    r_args, r_in, r_out, r_shapes = _rider_specs(riders)
    slab = lambda i, j: (((i + 1) % gi) * n_slabs + jnp.minimum(j, n_slabs - 1), 0)
    outs = pl.pallas_call(
        functools.partial(_swiglu_kernel, riders=riders, n_slabs=n_slabs),
        grid=(gi, gj),
        in_specs=[pl.BlockSpec(memory_space=pl.ANY),
                  pl.BlockSpec((d, tn), lambda i, j: (0, j)),
                  pl.BlockSpec((d, tn), lambda i, j: (0, j)),
                  pl.BlockSpec((rows, d), slab),
                  pl.BlockSpec((1, d), lambda i, j: (0, 0))] + r_in,
        out_specs=[pl.BlockSpec((tm, tn), lambda i, j: (i, j))] + r_out,
        out_shape=[jax.ShapeDtypeStruct((s, f), BF16)] + r_shapes,
        scratch_shapes=[pltpu.VMEM((tm, d), BF16), pltpu.VMEM((tm, d), BF16)],
        compiler_params=_params("arbitrary", "arbitrary"),
        name="swiglu_up",
    )(hn0, w_gate, w_up, h, gain.reshape(1, d), *r_args)
    return (outs[0], *_rider_results(riders, outs[1:]))


def kernel(x, norm1_g, w_in, gm_ln_g, gm_ln_b, gm_w_s, gm_b_s, q_gain, k_gain, na_rpb,
           w_o_gm, w_o_na, w_out, norm2_g, w_ff_gate, w_ff_up, w_ff_down):
    batch, seq, d_model = x.shape
    depth = w_in.shape[0]
    gm_width = gm_ln_g.shape[1]
    na_width = w_o_na.shape[1]
    heads = na_width // NA_HEAD_DIM
    outs = []
    for bi in range(batch):
        h = x.reshape(seq, d_model) if batch == 1 else x[bi]
        for layer in range(depth):
            cast = lambda *ws: functools.partial(_CastRider, ws)
            v_col = 2 * gm_width + 2 * na_width
            w_v = w_in[layer][:, v_col:v_col + na_width].astype(BF16)
            v, xn, (w_uv,) = _first_project(
                h, norm1_g[layer], w_v,
                riders=(functools.partial(_CastRider, (w_in[layer],), n_cols=2 * gm_width),))
            uv, (w_in_l,) = _project(xn, w_uv, 0, 2 * gm_width, "gelu", riders=(cast(w_in[layer]),))
            gains = jnp.concatenate([jnp.tile(q_gain[layer] * np.float32(NA_HEAD_DIM ** -0.5), heads),
                                     jnp.tile(k_gain[layer], heads)]).reshape(1, 2 * na_width)
            qk, (w_oa, w_ob, w_o) = _project(xn, w_in_l, 2 * gm_width, 2 * na_width, "headnorm", gains,
                                             riders=(cast(w_o_gm[layer], w_o_na[layer], w_out[layer]),))
            gmlp = functools.partial(
                _GmlpRider, (uv, gm_ln_g[layer], gm_ln_b[layer], gm_w_s[layer], gm_b_s[layer]))
            sg, (w_gate, w_up), y_a = _project(
                xn, w_in_l, 2 * gm_width + 3 * na_width, 2 * d_model, "sigmoid",
                riders=(cast(w_ff_gate[layer], w_ff_up[layer]), gmlp))
            y_b = _neighbourhood_attention(qk, v, na_rpb[layer])
            merged = _merge(y_a, y_b, w_oa, w_ob, sg)
            h = _matmul_residual(merged, w_o, h,
                                 (1024, 512, 256, 128), (1024, 512, 256, 128), "out_proj")
            act, (w_down,) = _swiglu_up(h, norm2_g[layer], w_gate, w_up, riders=(cast(w_ff_down[layer]),))
            h = _matmul_residual_wide_k(act, w_down, h, "ffn_down")
        outs.append(h)
    if batch == 1:
        return outs[0].reshape(1, seq, d_model)
    return jnp.stack(outs)
```
